```python
import math
import jax, jax.numpy as jnp
from jax import lax
import numpy as np

D_MODEL = 1024
BATCH = 2
SEQ = 8192
DEPTH = 4
DEC_BATCH = 128
DEC_SEQ = 1
PAST_LEN = 2048
PAGE_SIZE = 128

N_HEADS = 16
HEAD_DIM = D_MODEL // N_HEADS
D_FF = ((8 * D_MODEL // 3 + 127) // 128) * 128
CONV_W = 3
MOBA_BLOCK = 256
MOBA_TOP_K = 3
Q_BLOCK = 128
ROPE_THETA = 10000.0
N_MIXERS = 2
ALPHA = (2.0 * DEPTH) ** 0.25
BETA = (8.0 * DEPTH) ** -0.25
LN_EPS = 1e-5

kernel_name = 'hybrid_stickbreak_moba_convffn_step'


def layer_norm(x, g, b):
    xf = x.astype(jnp.float32)
    mu = xf.mean(-1, keepdims=True)
    var = jnp.square(xf - mu).mean(-1, keepdims=True)
    y = (xf - mu) * lax.rsqrt(var + LN_EPS) * g.astype(jnp.float32) + b.astype(jnp.float32)
    return y.astype(x.dtype)


def rope(x, pos):
    half = HEAD_DIM // 2
    inv = ROPE_THETA ** (-jnp.arange(half, dtype=jnp.float32) / half)
    ang = pos.astype(jnp.float32)[:, None] * inv[None, :]
    cos = jnp.cos(ang)[None, :, None, :]
    sin = jnp.sin(ang)[None, :, None, :]
    xf = x.astype(jnp.float32)
    x1, x2 = xf[..., :half], xf[..., half:]
    return jnp.concatenate([x1 * cos - x2 * sin, x2 * cos + x1 * sin], -1).astype(x.dtype)


def project_qkv(x, w_qkv_l):
    B, T, _ = x.shape
    q, k, v = jnp.split(x @ w_qkv_l, 3, axis=-1)
    shp = (B, T, N_HEADS, HEAD_DIM)
    return q.reshape(shp), k.reshape(shp), v.reshape(shp)


def sb_attend(q, k, v, q_pos, k_pos):
    scale = 1.0 / math.sqrt(HEAD_DIM)
    z = jnp.einsum('bqhd,bkhd->bhqk', q.astype(jnp.float32), k.astype(jnp.float32)) * scale
    causal = (k_pos[None, :] < q_pos[:, None])[None, None]
    log_one_minus = jnp.where(causal, jax.nn.log_sigmoid(-z), 0.0)
    rev = lax.cumsum(log_one_minus, axis=3, reverse=True)
    a = jnp.where(causal, jnp.exp(z + rev), 0.0)
    o = jnp.einsum('bhqk,bkhd->bqhd', a, v.astype(jnp.float32))
    return o.astype(v.dtype)


def to_blocks(x):
    B, T, H, Dh = x.shape
    nkb = max(-(-T // MOBA_BLOCK), MOBA_TOP_K)
    xp = jnp.pad(x, ((0, 0), (0, nkb * MOBA_BLOCK - T), (0, 0), (0, 0)))
    return xp.reshape(B, nkb, MOBA_BLOCK, H, Dh)


def moba_attend(q, kb, vb, kmean, q_pos):
    B, Tq, H, Dh = q.shape
    nkb = kb.shape[1]
    n_sel = MOBA_TOP_K + 1
    scale = 1.0 / math.sqrt(Dh)
    qf = q.astype(jnp.float32)
    own = q_pos // MOBA_BLOCK
    gate = jnp.einsum('bqhd,bnhd->bhqn', qf, kmean)
    past = jnp.arange(nkb)[None, :] < own[:, None]
    gate = jnp.where(past[None, None], gate, -jnp.inf)
    _, sel = lax.top_k(gate, MOBA_TOP_K)
    idx = jnp.concatenate(
        [sel.astype(jnp.int32), jnp.broadcast_to(own[None, None, :, None], (B, H, Tq, 1)).astype(jnp.int32)], -1)
    flat = idx.reshape(B, H, Tq * n_sel)
    kbt = kb.transpose(0, 3, 1, 2, 4)
    vbt = vb.transpose(0, 3, 1, 2, 4)
    b_i = jnp.arange(B)[:, None, None]
    h_i = jnp.arange(H)[None, :, None]
    kg = kbt[b_i, h_i, flat].reshape(B, H, Tq, n_sel, MOBA_BLOCK, Dh)
    vg = vbt[b_i, h_i, flat].reshape(B, H, Tq, n_sel, MOBA_BLOCK, Dh)
    s = jnp.einsum('bqhd,bhqnkd->bhqnk', qf, kg.astype(jnp.float32)) * scale
    kpos = idx[..., None] * MOBA_BLOCK + jnp.arange(MOBA_BLOCK)
    sel_ok = jnp.arange(MOBA_TOP_K)[None, :] < own[:, None]
    block_ok = jnp.concatenate([sel_ok, jnp.ones((Tq, 1), bool)], -1)
    mask = block_ok[None, None, :, :, None] & (kpos <= q_pos[None, None, :, None, None])
    s = jnp.where(mask, s, -jnp.inf).reshape(B, H, Tq, n_sel * MOBA_BLOCK)
    p = jax.nn.softmax(s, axis=-1).reshape(B, H, Tq, n_sel, MOBA_BLOCK)
    o = jnp.einsum('bhqnk,bhqnkd->bqhd', p, vg.astype(jnp.float32))
    return o.astype(q.dtype)


def sweep_query_blocks(fn, q):
    B, T, H, Dh = q.shape
    nqb = T // Q_BLOCK
    b_ids = jnp.repeat(jnp.arange(B, dtype=jnp.int32), nqb)
    qb_ids = jnp.tile(jnp.arange(nqb, dtype=jnp.int32), B)

    def body(ids):
        b, qb = ids
        q_b = lax.dynamic_index_in_dim(q, b, 0, keepdims=True)
        q_blk = lax.dynamic_slice_in_dim(q_b, qb * Q_BLOCK, Q_BLOCK, axis=1)
        q_pos = qb * Q_BLOCK + jnp.arange(Q_BLOCK, dtype=jnp.int32)
        return fn(q_blk, b, q_pos)[0]

    out = lax.map(body, (b_ids, qb_ids))
    return out.reshape(B, T, H, Dh)


def dyn_b(x, b):
    return lax.dynamic_index_in_dim(x, b, 0, keepdims=True)


def mixer_prompt(l, x, w_qkv_l, w_o_l):
    B, T, _ = x.shape
    q, k, v = project_qkv(x, w_qkv_l)
    pos = jnp.arange(T, dtype=jnp.int32)
    if l % N_MIXERS == 0:
        o = sweep_query_blocks(lambda qb_, b, qp: sb_attend(qb_, dyn_b(k, b), dyn_b(v, b), qp, pos), q)
    else:
        q = rope(q, pos)
        k = rope(k, pos)
        kb, vb = to_blocks(k), to_blocks(v)
        km = kb.astype(jnp.float32).mean(axis=2)
        o = sweep_query_blocks(
            lambda qb_, b, qp: moba_attend(qb_, dyn_b(kb, b), dyn_b(vb, b), dyn_b(km, b), qp), q)
    return o.reshape(B, T, D_MODEL) @ w_o_l, k, v


def mixer_sample(l, x, past_k, past_v, w_qkv_l, w_o_l):
    B, S, _ = x.shape
    P = past_k.shape[1]
    q, k, v = project_qkv(x, w_qkv_l)
    q_pos = P + jnp.arange(S, dtype=jnp.int32)
    k_pos = jnp.arange(P + S, dtype=jnp.int32)
    if l % N_MIXERS == 0:
        o = sb_attend(q, jnp.concatenate([past_k, k], 1), jnp.concatenate([past_v, v], 1), q_pos, k_pos)
    else:
        q = rope(q, q_pos)
        k = rope(k, q_pos)
        kb = to_blocks(jnp.concatenate([past_k, k], 1))
        vb = to_blocks(jnp.concatenate([past_v, v], 1))
        km = kb.astype(jnp.float32).mean(axis=2)
        o = moba_attend(q, kb, vb, km, q_pos)
    return o.reshape(B, S, D_MODEL) @ w_o_l, k, v


def conv_ffn(x, prev, w_in_l, conv_w_l, conv_b_l, w_down_l):
    T = x.shape[1]
    a, g = jnp.split(x @ w_in_l, 2, axis=-1)
    a_ext = jnp.concatenate([prev, a], axis=1)
    c = conv_b_l + sum(conv_w_l[i] * a_ext[:, i:i + T] for i in range(CONV_W))
    h = jax.nn.gelu(c) * g
    return h @ w_down_l, a_ext[:, -(CONV_W - 1):]


def setup_inputs(seed: int = 0) -> dict:
    key = jax.random.key(seed)
    ks = jax.random.split(key, 20)
    f32 = jnp.float32
    n_pages = PAST_LEN // PAGE_SIZE
    n_used = DEC_BATCH * n_pages
    n_pool = n_used + max(1, n_used // 4)
    D = D_MODEL
    x_prompt = jax.random.normal(ks[0], (BATCH, SEQ, D), f32)
    x_sample = jax.random.normal(ks[1], (DEC_BATCH, DEC_SEQ, D), f32)
    cache_k = jax.random.normal(ks[2], (DEPTH, n_pool, PAGE_SIZE, N_HEADS, HEAD_DIM), f32)
    cache_v = jax.random.normal(ks[3], (DEPTH, n_pool, PAGE_SIZE, N_HEADS, HEAD_DIM), f32)
    state_conv = jax.random.normal(ks[4], (DEPTH, DEC_BATCH, CONV_W - 1, D_FF), f32)
    page_table = jax.random.permutation(ks[5], n_pool)[:n_used].reshape(DEC_BATCH, n_pages).astype(jnp.int32)
    sd = D ** -0.5
    wq = jax.random.normal(ks[6], (DEPTH, D, D), f32) * sd
    wk = jax.random.normal(ks[7], (DEPTH, D, D), f32) * sd
    wv = jax.random.normal(ks[8], (DEPTH, D, D), f32) * sd * BETA
    w_qkv = jnp.concatenate([wq, wk, wv], axis=-1)
    w_o = jax.random.normal(ks[9], (DEPTH, D, D), f32) * sd * BETA
    ln1_g = 1.0 + 0.02 * jax.random.normal(ks[10], (DEPTH, D), f32)
    ln1_b = 0.02 * jax.random.normal(ks[11], (DEPTH, D), f32)
    w_in = jax.random.normal(ks[12], (DEPTH, D, 2 * D_FF), f32) * sd * BETA
    conv_w = jax.random.normal(ks[13], (DEPTH, CONV_W, D_FF), f32) * (CONV_W ** -0.5)
    conv_b = 0.02 * jax.random.normal(ks[14], (DEPTH, D_FF), f32)
    w_down = jax.random.normal(ks[15], (DEPTH, D_FF, D), f32) * (D_FF ** -0.5) * BETA
    ln2_g = 1.0 + 0.02 * jax.random.normal(ks[16], (DEPTH, D), f32)
    ln2_b = 0.02 * jax.random.normal(ks[17], (DEPTH, D), f32)
    return {'x_prompt': x_prompt, 'x_sample': x_sample, 'cache_k': cache_k, 'cache_v': cache_v,
            'state_conv': state_conv, 'page_table': page_table, 'w_qkv': w_qkv, 'w_o': w_o,
            'ln1_g': ln1_g, 'ln1_b': ln1_b, 'w_in': w_in, 'conv_w': conv_w, 'conv_b': conv_b,
            'w_down': w_down, 'ln2_g': ln2_g, 'ln2_b': ln2_b}


def reference(x_prompt, x_sample, cache_k, cache_v, state_conv, page_table, w_qkv, w_o,
              ln1_g, ln1_b, w_in, conv_w, conv_b, w_down, ln2_g, ln2_b):
    xp, xs = x_prompt, x_sample
    n_seq = page_table.shape[0]
    kp_l, vp_l, cp_l, ks_l, vs_l, cs_l = [], [], [], [], [], []
    for l in range(DEPTH):
        a_p, kp, vp = mixer_prompt(l, xp, w_qkv[l], w_o[l])
        xp = layer_norm(ALPHA * xp + a_p, ln1_g[l], ln1_b[l])
        past_k = cache_k[l][page_table].reshape(n_seq, -1, N_HEADS, HEAD_DIM)
        past_v = cache_v[l][page_table].reshape(n_seq, -1, N_HEADS, HEAD_DIM)
        a_s, ksn, vsn = mixer_sample(l, xs, past_k, past_v, w_qkv[l], w_o[l])
        xs = layer_norm(ALPHA * xs + a_s, ln1_g[l], ln1_b[l])
        prev_p = jnp.zeros((xp.shape[0], CONV_W - 1, D_FF), xp.dtype)
        f_p, cp = conv_ffn(xp, prev_p, w_in[l], conv_w[l], conv_b[l], w_down[l])
        xp = layer_norm(ALPHA * xp + f_p, ln2_g[l], ln2_b[l])
        f_s, cs = conv_ffn(xs, state_conv[l].astype(xs.dtype), w_in[l], conv_w[l], conv_b[l], w_down[l])
        xs = layer_norm(ALPHA * xs + f_s, ln2_g[l], ln2_b[l])
        kp_l.append(kp); vp_l.append(vp); cp_l.append(cp)
        ks_l.append(ksn); vs_l.append(vsn); cs_l.append(cs)
    new_k_prompt = jnp.stack(kp_l)
    new_v_prompt = jnp.stack(vp_l)
    new_conv_prompt = jnp.stack(cp_l)
    new_k_sample = jnp.stack(ks_l)
    new_v_sample = jnp.stack(vs_l)
    new_conv_sample = jnp.stack(cs_l)
    return (xp, xs, new_k_prompt, new_v_prompt, new_conv_prompt, new_k_sample, new_v_sample, new_conv_sample)
```

```python
import functools
import math

import jax
import jax.numpy as jnp
from jax import lax
from jax.experimental import pallas as pl
from jax.experimental.pallas import tpu as pltpu

N_HEADS = 16
MOBA_BLOCK = 256
MOBA_TOP_K = 3
ROPE_THETA = 10000.0
LN_EPS = 1e-5
N_MIXERS = 2
CONV_W = 3

LANES = 128
SUBLANES = 8
ROW_TILE = 512
FF_CHUNK = 256
ATTN_TILE = MOBA_BLOCK
DECODE_PAGES_PER_STEP = 4
VMEM_LIMIT = 56 * 1024 * 1024
MASK_VALUE = -1e30

F32 = jnp.float32
BF16 = jnp.bfloat16


def _dot(a, b):
    return jnp.dot(a, b, preferred_element_type=F32)


def _dot_nt(a, b):
    return lax.dot_general(a, b, (((1,), (1,)), ((), ())), preferred_element_type=F32)


def _layer_norm(y, g, b):
    mu = jnp.mean(y, axis=-1, keepdims=True)
    d = y - mu
    var = jnp.mean(d * d, axis=-1, keepdims=True)
    return d * lax.rsqrt(var + LN_EPS) * g + b


def _log_one_minus_sigmoid(z):
    return -(jnp.maximum(z, 0.0) + jnp.log(1.0 + jnp.exp(-jnp.abs(z))))


def _hi_lo(x):
    hi = x.astype(BF16)
    return hi, (x - hi.astype(F32)).astype(BF16)


def _rope(x, cos, sin_signed):
    lane = lax.broadcasted_iota(jnp.int32, cos.shape, 1)
    first_half = (lane % 64) < 32
    cols = []
    for c in range(x.shape[1] // LANES):
        xc = x[:, c * LANES:(c + 1) * LANES]
        partner = jnp.where(first_half, pltpu.roll(xc, 96, 1), pltpu.roll(xc, 32, 1))
        cols.append(xc * cos + partner * sin_signed)
    return jnp.concatenate(cols, axis=1)


def _rope_tables(pos, head_dim):
    half = head_dim // 2
    inv = ROPE_THETA ** (-jnp.arange(half, dtype=F32) / half)
    ang = pos.astype(F32)[:, None] * inv[None, :]
    cos, sin = jnp.cos(ang), jnp.sin(ang)
    reps = LANES // head_dim
    return jnp.tile(jnp.concatenate([cos, cos], -1), (1, reps)), jnp.tile(jnp.concatenate([-sin, sin], -1), (1, reps))


def _qkv_kernel(*refs, rope, moba, decode, d, scale):
    if rope:
        x_ref, w_ref, cos_ref, sin_ref = refs[:4]
        outs = refs[4:]
    else:
        x_ref, w_ref = refs[:2]
        outs = refs[2:]
    xb = x_ref[...].astype(BF16)
    q = _dot(xb, w_ref[:, 0:d])
    k = _dot(xb, w_ref[:, d:2 * d])
    v = _dot(xb, w_ref[:, 2 * d:3 * d])
    if rope:
        cos, sin = cos_ref[...], sin_ref[...]
        q = _rope(q, cos, sin)
        k = _rope(k, cos, sin)
    q = q * scale
    if decode:
        q_ref, k_ref, v_ref = outs
        q_ref[...] = q
        k_ref[...] = k
        v_ref[...] = v
        return
    if moba:
        qhi_ref, qlo_ref, k_ref, v_ref, kb_ref, vb_ref, km_ref = outs
    else:
        qhi_ref, k_ref, v_ref, kb_ref, vb_ref = outs
    q_hi = q.astype(BF16)
    qhi_ref[...] = q_hi
    k_ref[...] = k
    v_ref[...] = v
    kb_ref[...] = k.astype(BF16)
    vb_ref[...] = v.astype(BF16)
    if moba:
        qlo_ref[...] = (q - q_hi.astype(F32)).astype(BF16)
        for j in range(k.shape[0] // MOBA_BLOCK):
            blk = k[j * MOBA_BLOCK:(j + 1) * MOBA_BLOCK]
            km_ref[j] = jnp.sum(blk, axis=0, keepdims=True) * (1.0 / MOBA_BLOCK)


def _qkv_proj(x, w_bf16, tables, *, rope, moba, decode, tm):
    m, d = x.shape
    head_dim = d // N_HEADS
    grid = (m // tm,)
    row = lambda i: (i, 0)
    in_specs = [pl.BlockSpec((tm, d), row),
                pl.BlockSpec((d, 3 * d), lambda i: (0, 0), pipeline_mode=pl.Buffered(1))]
    args = [x, w_bf16]
    if rope:
        cos, sin = tables
        nt = cos.shape[0] // tm
        tab = pl.BlockSpec((tm, LANES), lambda i: (i % nt, 0))
        in_specs += [tab, tab]
        args += [cos, sin]
    f32_out = jax.ShapeDtypeStruct((m, d), F32)
    bf_out = jax.ShapeDtypeStruct((m, d), BF16)
    spec = pl.BlockSpec((tm, d), row)
    if decode:
        out_shape = [f32_out] * 3
        out_specs = [spec] * 3
    else:
        out_shape = [bf_out] + ([bf_out] if moba else []) + [f32_out, f32_out, bf_out, bf_out]
        out_specs = [spec] * len(out_shape)
        if moba:
            nb = tm // MOBA_BLOCK
            out_shape.append(jax.ShapeDtypeStruct((m // MOBA_BLOCK, 1, d), F32))
            out_specs.append(pl.BlockSpec((nb, 1, d), lambda i: (i, 0, 0)))
    kern = functools.partial(_qkv_kernel, rope=rope, moba=moba, decode=decode, d=d,
                             scale=1.0 / math.sqrt(head_dim))
    return pl.pallas_call(
        kern, grid=grid, in_specs=in_specs, out_specs=out_specs, out_shape=out_shape,
        compiler_params=pltpu.CompilerParams(dimension_semantics=("arbitrary",),
                                             vmem_limit_bytes=VMEM_LIMIT),
        name="qkv_proj_decode" if decode else ("qkv_proj_moba" if moba else "qkv_proj_sb"),
    )(*args)


def _head_masks(tq):
    lane = lax.broadcasted_iota(jnp.int32, (tq, LANES), 1)
    return lane < (LANES // 2)


def _sb_prompt_kernel(q_ref, k_ref, v_ref, o_ref, *, tile):
    qi = pl.program_id(2)
    q = q_ref[...]
    lo_half = _head_masks(tile)
    row = lax.broadcasted_iota(jnp.int32, (tile, tile), 0)
    col = lax.broadcasted_iota(jnp.int32, (tile, tile), 1)
    causal = col < row
    suffix = (row >= col).astype(BF16)
    zero_q = jnp.zeros_like(q)

    def one_tile(qh, kt, carry, acc, masked):
        start = pl.multiple_of(kt * tile, tile)
        k_t = k_ref[pl.ds(start, tile), :]
        v_t = v_ref[pl.ds(start, tile), :]
        z = _dot_nt(qh, k_t)
        lg = _log_one_minus_sigmoid(z)
        if masked:
            lg = jnp.where(causal, lg, 0.0)
        lg_hi, lg_lo = _hi_lo(lg)
        rev = _dot(lg_hi, suffix) + _dot(lg_lo, suffix)
        a = jnp.exp(z + rev + carry)
        if masked:
            a = jnp.where(causal, a, 0.0)
        acc = acc + _dot(a.astype(BF16), v_t)
        carry = carry + jnp.sum(lg, axis=1, keepdims=True)
        return carry, acc

    accs = []
    for h in range(2):
        qh = jnp.where(lo_half if h == 0 else jnp.logical_not(lo_half), q, zero_q)
        carry = jnp.zeros((tile, 1), F32)
        acc = jnp.zeros((tile, LANES), F32)
        carry, acc = one_tile(qh, qi, carry, acc, True)

        def body(j, ca, qh=qh):
            return one_tile(qh, qi - 1 - j, ca[0], ca[1], False)

        carry, acc = lax.fori_loop(0, qi, body, (carry, acc))
        accs.append(acc)
    o_ref[...] = jnp.where(lo_half, accs[0], accs[1]).astype(o_ref.dtype)


def _moba_prompt_kernel(qhi_ref, qlo_ref, k_ref, v_ref, km_ref, o_ref, *, tile, n_blocks):
    qi = pl.program_id(2)
    q_hi = qhi_ref[...]
    q_lo = qlo_ref[...]
    lo_half = _head_masks(tile)
    row = lax.broadcasted_iota(jnp.int32, (tile, tile), 0)
    col = lax.broadcasted_iota(jnp.int32, (tile, tile), 1)
    causal = col <= row
    km_hi, km_lo = _hi_lo(km_ref[...])
    blk = lax.broadcasted_iota(jnp.int32, (tile, n_blocks), 1)
    zero_q = jnp.zeros_like(q_hi)
    k_own = k_ref[pl.ds(pl.multiple_of(qi * tile, tile), tile), :]
    v_own = v_ref[pl.ds(pl.multiple_of(qi * tile, tile), tile), :]

    accs = []
    for h in range(2):
        hm = lo_half if h == 0 else jnp.logical_not(lo_half)
        qh = jnp.where(hm, q_hi, zero_q)
        ql = jnp.where(hm, q_lo, zero_q)
        gate = _dot_nt(qh, km_hi) + _dot_nt(qh, km_lo) + _dot_nt(ql, km_hi)
        gate = jnp.where(blk < qi, gate, -jnp.inf)
        sel = []
        for r in range(MOBA_TOP_K):
            mx = jnp.max(gate, axis=1, keepdims=True)
            idx = jnp.min(jnp.where(gate == mx, blk, n_blocks), axis=1, keepdims=True)
            gate = jnp.where(blk == idx, -jnp.inf, gate)
            sel.append(jnp.where(r < qi, idx, -1))

        s = jnp.where(causal, _dot_nt(qh, k_own), MASK_VALUE)
        m = jnp.max(s, axis=1, keepdims=True)
        p = jnp.exp(s - m)
        l = jnp.sum(p, axis=1, keepdims=True)
        acc = _dot(p.astype(BF16), v_own)

        def body(n, mla, qh=qh, sel=sel):
            m, l, acc = mla
            start = pl.multiple_of(n * tile, tile)
            k_t = k_ref[pl.ds(start, tile), :]
            v_t = v_ref[pl.ds(start, tile), :]
            chosen = (sel[0] == n) | (sel[1] == n) | (sel[2] == n)
            s = jnp.where(chosen, _dot_nt(qh, k_t), MASK_VALUE)
            m_new = jnp.maximum(m, jnp.max(s, axis=1, keepdims=True))
            alpha = jnp.exp(m - m_new)
            p = jnp.exp(s - m_new)
            l = alpha * l + jnp.sum(p, axis=1, keepdims=True)
            acc = alpha * acc + _dot(p.astype(BF16), v_t)
            return m_new, l, acc

        m, l, acc = lax.fori_loop(0, qi, body, (m, l, acc))
        accs.append(acc / l)
    o_ref[...] = jnp.where(lo_half, accs[0], accs[1]).astype(o_ref.dtype)


def _prompt_attention(q_parts, kb, vb, kmean, *, batch, seq, moba):
    m, d = kb.shape
    tile = ATTN_TILE
    nq = seq // tile
    n_pairs = d // LANES
    grid = (batch, n_pairs, nq)
    q_spec = pl.BlockSpec((tile, LANES), lambda b, p, i: (b * nq + i, p))
    kv_spec = pl.BlockSpec((seq, LANES), lambda b, p, i: (b, p))
    in_specs = [q_spec] * len(q_parts) + [kv_spec, kv_spec]
    args = list(q_parts) + [kb, vb]
    if moba:
        n_blocks = seq // MOBA_BLOCK
        in_specs.append(pl.BlockSpec((None, n_blocks, LANES), lambda b, p, i: (b, 0, p)))
        args.append(kmean)
        kern = functools.partial(_moba_prompt_kernel, tile=tile, n_blocks=n_blocks)
    else:
        kern = functools.partial(_sb_prompt_kernel, tile=tile)
    return pl.pallas_call(
        kern, grid=grid, in_specs=in_specs, out_specs=q_spec,
        out_shape=jax.ShapeDtypeStruct((m, d), BF16),
        compiler_params=pltpu.CompilerParams(dimension_semantics=("arbitrary",) * 3,
                                             vmem_limit_bytes=VMEM_LIMIT),
        name="moba_prompt" if moba else "sb_prompt",
    )(*args)


def _decode_kernel(pt_ref, q_ref, kn_ref, vn_ref, *refs, pps, page, moba, n_steps):
    del pt_ref
    kt_refs = refs[:pps]
    vt_refs = refs[pps:2 * pps]
    o_ref = refs[2 * pps]
    scratch = refs[2 * pps + 1:]
    d = q_ref.shape[-1]
    hd = d // N_HEADS
    s = pl.program_id(1)
    heads = [slice(h * hd, (h + 1) * hd) for h in range(N_HEADS)]

    def column(row_ref):
        return jnp.broadcast_to(row_ref[0], (page, d)).T

    def scores(kt, qcol):
        return jnp.concatenate(
            [jnp.sum(kt[sl, :] * qcol[sl, :], axis=0, keepdims=True) for sl in heads], axis=0)

    def finish(out_t):
        o_ref[0] = jnp.sum(out_t.T, axis=0, keepdims=True)

    if not moba:
        qcol_ref, carry_ref, acc_ref = scratch

        @pl.when(s == 0)
        def _():
            qcol_ref[...] = column(q_ref)
            carry_ref[...] = jnp.zeros_like(carry_ref)
            acc_ref[...] = jnp.zeros_like(acc_ref)

        r = lax.broadcasted_iota(jnp.int32, (page, page), 0)
        c = lax.broadcasted_iota(jnp.int32, (page, page), 1)
        suffix = (r >= c).astype(BF16)
        carry = carry_ref[...]
        for j in reversed(range(pps)):
            z = scores(kt_refs[j], qcol_ref)
            lg = _log_one_minus_sigmoid(z)
            lg_hi, lg_lo = _hi_lo(lg)
            rev = _dot(lg_hi, suffix) + _dot(lg_lo, suffix)
            a = jnp.exp(z + rev + carry)
            for h, sl in enumerate(heads):
                acc_ref[sl, :] += vt_refs[j][sl, :] * a[h:h + 1, :]
            carry = carry + jnp.sum(lg, axis=1, keepdims=True)
        carry_ref[...] = carry

        @pl.when(s == n_steps - 1)
        def _():
            finish(acc_ref[...])
        return

    qcol_ref, m_ref, l_ref, g_ref, acc_ref = scratch
    pages_per_block = MOBA_BLOCK // page
    n_past = n_steps * pps // pages_per_block

    @pl.when(s == 0)
    def _():
        qcol_ref[...] = column(q_ref)

    for jb in range(pps // pages_per_block):
        n = s * (pps // pages_per_block) + jb
        ps = [jb * pages_per_block + i for i in range(pages_per_block)]
        zs = [scores(kt_refs[p], qcol_ref) for p in ps]
        m_n = functools.reduce(jnp.maximum, [jnp.max(z, axis=1, keepdims=True) for z in zs])
        es = [jnp.exp(z - m_n) for z in zs]
        l_n = functools.reduce(jnp.add, [jnp.sum(e, axis=1, keepdims=True) for e in es])
        g_n = functools.reduce(jnp.add, [jnp.sum(z, axis=1, keepdims=True) for z in zs]) * (1.0 / MOBA_BLOCK)
        m_ref[n] = jnp.broadcast_to(m_n, (N_HEADS, page))
        l_ref[n] = jnp.broadcast_to(l_n, (N_HEADS, page))
        g_ref[n] = jnp.broadcast_to(g_n, (N_HEADS, page))
        for h, sl in enumerate(heads):
            acc_ref[n, sl, :] = functools.reduce(
                jnp.add, [vt_refs[p][sl, :] * e[h:h + 1, :] for p, e in zip(ps, es)])

    @pl.when(s == n_steps - 1)
    def _():
        g = [g_ref[n] for n in range(n_past)]
        m = [m_ref[n] for n in range(n_past)]
        chosen = []
        for n in range(n_past):
            rank = jnp.zeros((N_HEADS, page), jnp.int32)
            for o in range(n_past):
                if o != n:
                    ahead = (g[o] >= g[n]) if o < n else (g[o] > g[n])
                    rank = rank + ahead.astype(jnp.int32)
            chosen.append(rank < MOBA_TOP_K)
        s_own = scores(column(kn_ref), qcol_ref)
        m_all = functools.reduce(jnp.maximum, [jnp.where(ch, mm, MASK_VALUE) for ch, mm in zip(chosen, m)] + [s_own])
        w = [jnp.where(ch, jnp.exp(mm - m_all), 0.0) for ch, mm in zip(chosen, m)]
        w_own = jnp.exp(s_own - m_all)
        denom = functools.reduce(jnp.add, [wn * l_ref[n] for n, wn in enumerate(w)]) + w_own
        inv = 1.0 / denom
        lane = lax.broadcasted_iota(jnp.int32, (N_HEADS, page), 1)
        w_own = jnp.where(lane == 0, w_own * inv, 0.0)
        w = [wn * inv for wn in w]
        vcol = column(vn_ref)
        parts = []
        for h, sl in enumerate(heads):
            part = vcol[sl, :] * w_own[h:h + 1, :]
            for n in range(n_past):
                part = part + acc_ref[n, sl, :] * w[n][h:h + 1, :]
            parts.append(part)
        finish(jnp.concatenate(parts, axis=0))


def _decode_attention(layer, q, k_new, v_new, cache_kt, cache_vt, page_table, *, moba):
    db, _, d = q.shape
    n_pages = page_table.shape[1]
    page = cache_kt.shape[3]
    pps = DECODE_PAGES_PER_STEP
    n_steps = n_pages // pps
    vec = pl.BlockSpec((1, 1, d), lambda b, s, pt: (b, 0, 0))

    def page_spec(j):
        if moba:
            idx = lambda b, s, pt: (layer, pt[b, s * pps + j], 0, 0)
        else:
            idx = lambda b, s, pt: (layer, pt[b, (n_steps - 1 - s) * pps + j], 0, 0)
        return pl.BlockSpec((None, None, d, page), idx)

    in_specs = [vec, vec, vec] + [page_spec(j) for j in range(pps)] * 2
    lanes_eq = pltpu.VMEM((N_HEADS, page), F32)
    if moba:
        n_past = n_pages * page // MOBA_BLOCK
        stats = pltpu.VMEM((n_past, N_HEADS, page), F32)
        scratch = [pltpu.VMEM((d, page), F32), stats, stats, stats, pltpu.VMEM((n_past, d, page), F32)]
    else:
        scratch = [pltpu.VMEM((d, page), F32), lanes_eq, pltpu.VMEM((d, page), F32)]
    kern = functools.partial(_decode_kernel, pps=pps, page=page, moba=moba, n_steps=n_steps)
    return pl.pallas_call(
        kern,
        grid_spec=pltpu.PrefetchScalarGridSpec(
            num_scalar_prefetch=1, grid=(db, n_steps), in_specs=in_specs, out_specs=vec,
            scratch_shapes=scratch),
        out_shape=jax.ShapeDtypeStruct((db, 1, d), F32),
        compiler_params=pltpu.CompilerParams(dimension_semantics=("arbitrary", "arbitrary"),
                                             vmem_limit_bytes=VMEM_LIMIT),
        name="moba_decode" if moba else "sb_decode",
    )(page_table, q, k_new, v_new, *([cache_kt] * pps), *([cache_vt] * pps))


def _gelu_tanh(x):
    return 0.5 * x * (1.0 + jnp.tanh(math.sqrt(2.0 / math.pi) * (x + 0.044715 * (x * x * x))))


def _post_kernel(*refs, decode, alpha, d_ff, tiles_per_seq):
    if decode:
        (x_ref, o_ref, wo_ref, g1_ref, b1_ref, win_ref, cw_ref, cb_ref, wd_ref, g2_ref, b2_ref,
         st_ref, y_ref, cs_ref, h_ref) = refs
    else:
        (x_ref, o_ref, wo_ref, g1_ref, b1_ref, win_ref, cw_ref, cb_ref, wd_ref, g2_ref, b2_ref,
         y_ref, cs_ref, h_ref, tail_ref) = refs
        i = pl.program_id(0)

        @pl.when(i % tiles_per_seq == 0)
        def _():
            tail_ref[...] = jnp.zeros_like(tail_ref)

    x = x_ref[...]
    tm = x.shape[0]
    attn = _dot(o_ref[...].astype(BF16), wo_ref[...])
    x1 = _layer_norm(alpha * x + attn, g1_ref[...], b1_ref[...])
    x1b = x1.astype(BF16)
    for c in range(d_ff // FF_CHUNK):
        lo, hi = c * FF_CHUNK, (c + 1) * FF_CHUNK
        a = _dot(x1b, win_ref[:, lo:hi])
        g = _dot(x1b, win_ref[:, d_ff + lo:d_ff + hi])
        if decode:
            prev2 = st_ref[:, lo:hi]
            prev1 = st_ref[:, d_ff + lo:d_ff + hi]
            cs_ref[:, lo:hi] = prev1
            cs_ref[:, d_ff + lo:d_ff + hi] = a
        else:
            ext = jnp.concatenate([tail_ref[:, lo:hi], a], axis=0)
            prev1 = ext[SUBLANES - 1:SUBLANES - 1 + tm]
            prev2 = ext[SUBLANES - 2:SUBLANES - 2 + tm]
            last = a[tm - SUBLANES:]
            tail_ref[:, lo:hi] = last
            cs_ref[:, lo:hi] = last
        conv = cb_ref[:, lo:hi] + cw_ref[0:1, lo:hi] * prev2 + cw_ref[1:2, lo:hi] * prev1 + cw_ref[2:3, lo:hi] * a
        h_ref[:, lo:hi] = (_gelu_tanh(conv) * g).astype(BF16)
    f = _dot(h_ref[...], wd_ref[...])
    y_ref[...] = _layer_norm(alpha * x1 + f, g2_ref[...], b2_ref[...])


def _post_attention(x, o, w_o, ln1_g, ln1_b, w_in, conv_w, conv_b, w_down, ln2_g, ln2_b, state,
                    *, alpha, tm, seq_rows):
    m, d = x.shape
    d_ff = w_down.shape[0]
    decode = state is not None
    grid = (m // tm,)
    row = lambda i: (i, 0)
    const = lambda i: (0, 0)

    def resident(shape):
        return pl.BlockSpec(shape, const, pipeline_mode=pl.Buffered(1))

    in_specs = [pl.BlockSpec((tm, d), row), pl.BlockSpec((tm, d), row),
                resident((d, d)), resident((1, d)), resident((1, d)),
                resident((d, 2 * d_ff)), resident((CONV_W, d_ff)), resident((1, d_ff)),
                resident((d_ff, d)), resident((1, d)), resident((1, d))]
    args = [x, o, w_o, ln1_g, ln1_b, w_in, conv_w, conv_b, w_down, ln2_g, ln2_b]
    scratch = [pltpu.VMEM((tm, d_ff), BF16)]
    if decode:
        in_specs.append(pl.BlockSpec((tm, 2 * d_ff), row))
        args.append(state)
        cs_shape = jax.ShapeDtypeStruct((m, 2 * d_ff), F32)
        cs_spec = pl.BlockSpec((tm, 2 * d_ff), row)
    else:
        tiles_per_seq = seq_rows // tm
        cs_shape = jax.ShapeDtypeStruct((m // seq_rows, SUBLANES, d_ff), F32)
        cs_spec = pl.BlockSpec((None, SUBLANES, d_ff), lambda i: (i // tiles_per_seq, 0, 0))
        scratch.append(pltpu.VMEM((SUBLANES, d_ff), F32))
    kern = functools.partial(_post_kernel, decode=decode, alpha=alpha, d_ff=d_ff,
                             tiles_per_seq=None if decode else seq_rows // tm)
    return pl.pallas_call(
        kern, grid=grid, in_specs=in_specs,
        out_specs=[pl.BlockSpec((tm, d), row), cs_spec],
        out_shape=[jax.ShapeDtypeStruct((m, d), F32), cs_shape],
        scratch_shapes=scratch,
        compiler_params=pltpu.CompilerParams(dimension_semantics=("arbitrary",),
                                             vmem_limit_bytes=VMEM_LIMIT),
        name="post_attn_ffn_decode" if decode else "post_attn_ffn",
    )(*args)


def kernel(x_prompt, x_sample, cache_k, cache_v, state_conv, page_table, w_qkv, w_o, ln1_g, ln1_b,
           w_in, conv_w, conv_b, w_down, ln2_g, ln2_b):
    batch, seq, d = x_prompt.shape
    db, dec_seq, _ = x_sample.shape
    depth, n_pool, page = cache_k.shape[:3]
    n_pages = page_table.shape[1]
    past_len = n_pages * page
    d_ff = w_down.shape[1]
    head_dim = d // N_HEADS
    assert dec_seq == 1 and 2 * head_dim == LANES and d_ff % FF_CHUNK == 0 and page == LANES
    assert seq % MOBA_BLOCK == 0 and seq % ROW_TILE == 0 and ROW_TILE % MOBA_BLOCK == 0
    assert past_len % MOBA_BLOCK == 0 and MOBA_BLOCK % page == 0 and db % SUBLANES == 0
    assert n_pages % DECODE_PAGES_PER_STEP == 0 and (DECODE_PAGES_PER_STEP * page) % MOBA_BLOCK == 0
    alpha = (2.0 * depth) ** 0.25

    xp = x_prompt.reshape(batch * seq, d)
    xs = x_sample.reshape(db, d)
    ckt = jnp.transpose(cache_k, (0, 1, 3, 4, 2)).reshape(depth, n_pool, d, page)
    cvt = jnp.transpose(cache_v, (0, 1, 3, 4, 2)).reshape(depth, n_pool, d, page)
    st = state_conv.reshape(depth, db, (CONV_W - 1) * d_ff)
    tables_p = _rope_tables(jnp.arange(seq, dtype=jnp.int32), head_dim)
    tables_s = _rope_tables(jnp.full((db,), past_len, jnp.int32), head_dim)
    row2 = lambda a: a.reshape(1, -1)

    kp_l, vp_l, cp_l, ks_l, vs_l, cs_l = [], [], [], [], [], []
    for l in range(depth):
        moba = l % N_MIXERS == 1
        wq = w_qkv[l].astype(BF16)
        post_w = (w_o[l].astype(BF16), row2(ln1_g[l]), row2(ln1_b[l]), w_in[l].astype(BF16), conv_w[l],
                  row2(conv_b[l]), w_down[l].astype(BF16), row2(ln2_g[l]), row2(ln2_b[l]))
        outs = _qkv_proj(xp, wq, tables_p, rope=moba, moba=moba, decode=False, tm=ROW_TILE)
        if moba:
            q_hi, q_lo, k, v, kb, vb, km = outs
            o = _prompt_attention((q_hi, q_lo), kb, vb, km.reshape(batch, seq // MOBA_BLOCK, d),
                                  batch=batch, seq=seq, moba=True)
        else:
            q_hi, k, v, kb, vb = outs
            o = _prompt_attention((q_hi,), kb, vb, None, batch=batch, seq=seq, moba=False)
        xp, tail = _post_attention(xp, o, *post_w, None, alpha=alpha, tm=ROW_TILE, seq_rows=seq)
        kp_l.append(k.reshape(batch, seq, N_HEADS, head_dim))
        vp_l.append(v.reshape(batch, seq, N_HEADS, head_dim))
        cp_l.append(tail[:, SUBLANES - (CONV_W - 1):, :])
        qs, ksn, vsn = _qkv_proj(xs, wq, tables_s, rope=moba, moba=moba, decode=True, tm=db)
        os_ = _decode_attention(l, qs.reshape(db, 1, d), ksn.reshape(db, 1, d), vsn.reshape(db, 1, d),
                                ckt, cvt, page_table, moba=moba)
        xs, cs = _post_attention(xs, os_.reshape(db, d), *post_w, st[l], alpha=alpha, tm=db, seq_rows=1)
        ks_l.append(ksn.reshape(db, 1, N_HEADS, head_dim))
        vs_l.append(vsn.reshape(db, 1, N_HEADS, head_dim))
        cs_l.append(cs.reshape(db, CONV_W - 1, d_ff))
    return (xp.reshape(batch, seq, d), xs.reshape(db, 1, d), jnp.stack(kp_l), jnp.stack(vp_l),
            jnp.stack(cp_l), jnp.stack(ks_l), jnp.stack(vs_l), jnp.stack(cs_l))
```

```python
import functools
import math

import jax
import jax.numpy as jnp
from jax import lax
from jax.experimental import pallas as pl
from jax.experimental.pallas import tpu as pltpu

N_HEADS = 16
MOBA_BLOCK = 256
MOBA_TOP_K = 3
ROPE_THETA = 10000.0
LN_EPS = 1e-5
N_MIXERS = 2
CONV_W = 3

LANES = 128
SUBLANES = 8
ROW_TILE = 512
FF_CHUNK = 256
ATTN_TILE = MOBA_BLOCK
MOBA_BLOCKS_PER_TRIP = 4
DECODE_PAGES_PER_STEP = 4
VMEM_LIMIT = 56 * 1024 * 1024
MASK_VALUE = -1e30

F32 = jnp.float32
BF16 = jnp.bfloat16


def _dot(a, b):
    return jnp.dot(a, b, preferred_element_type=F32)


def _dot_nt(a, b):
    return lax.dot_general(a, b, (((1,), (1,)), ((), ())), preferred_element_type=F32)


def _layer_norm(y, g, b):
    mu = jnp.mean(y, axis=-1, keepdims=True)
    d = y - mu
    var = jnp.mean(d * d, axis=-1, keepdims=True)
    return d * lax.rsqrt(var + LN_EPS) * g + b


def _log_one_minus_sigmoid(z):
    return -(jnp.maximum(z, 0.0) + jnp.log(1.0 + jnp.exp(-jnp.abs(z))))


def _hi_lo(x):
    hi = x.astype(BF16)
    return hi, (x - hi.astype(F32)).astype(BF16)


def _rope(x, cos, sin_signed):
    lane = lax.broadcasted_iota(jnp.int32, cos.shape, 1)
    first_half = (lane % 64) < 32
    cols = []
    for c in range(x.shape[1] // LANES):
        xc = x[:, c * LANES:(c + 1) * LANES]
        partner = jnp.where(first_half, pltpu.roll(xc, 96, 1), pltpu.roll(xc, 32, 1))
        cols.append(xc * cos + partner * sin_signed)
    return jnp.concatenate(cols, axis=1)


def _rope_tables(pos, head_dim):
    half = head_dim // 2
    inv = ROPE_THETA ** (-jnp.arange(half, dtype=F32) / half)
    ang = pos.astype(F32)[:, None] * inv[None, :]
    cos, sin = jnp.cos(ang), jnp.sin(ang)
    reps = LANES // head_dim
    return jnp.tile(jnp.concatenate([cos, cos], -1), (1, reps)), jnp.tile(jnp.concatenate([-sin, sin], -1), (1, reps))


def _qkv_kernel(*refs, rope, moba, decode, aliased, d, scale):
    n_in = 2 + (2 if rope else 0) + (2 if aliased else 0)
    x_ref, w_ref = refs[:2]
    outs = refs[n_in:]
    xb = x_ref[...].astype(BF16)
    q = _dot(xb, w_ref[:, 0:d])
    k = _dot(xb, w_ref[:, d:2 * d])
    v = _dot(xb, w_ref[:, 2 * d:3 * d])
    if rope:
        cos, sin = refs[2][...], refs[3][...]
        q = _rope(q, cos, sin)
        k = _rope(k, cos, sin)
    q = q * scale
    if decode:
        q_ref, k_ref, v_ref = outs
        q_ref[...] = q
        k_ref[...] = k
        v_ref[...] = v
        return
    if moba:
        qhi_ref, qlo_ref, kt_ref, vt_ref, kb_ref, vb_ref, km_ref = outs
    else:
        qhi_ref, kt_ref, vt_ref, kb_ref, vb_ref = outs
    q_hi = q.astype(BF16)
    qhi_ref[...] = q_hi
    kt_ref[...] = k.T
    vt_ref[...] = v.T
    kb_ref[...] = k.astype(BF16)
    vb_ref[...] = v.astype(BF16)
    if moba:
        qlo_ref[...] = (q - q_hi.astype(F32)).astype(BF16)
        for j in range(k.shape[0] // MOBA_BLOCK):
            blk = k[j * MOBA_BLOCK:(j + 1) * MOBA_BLOCK]
            km_ref[j] = jnp.sum(blk, axis=0, keepdims=True) * (1.0 / MOBA_BLOCK)


def _qkv_proj(x, w_bf16, tables, kv_bufs=None, *, rope, moba, decode, tm, layer=0, depth=1, seq=None):
    m, d = x.shape
    head_dim = d // N_HEADS
    grid = (m // tm,)
    row = lambda i: (i, 0)
    in_specs = [pl.BlockSpec((tm, d), row),
                pl.BlockSpec((d, 3 * d), lambda i: (0, 0), pipeline_mode=pl.Buffered(1))]
    args = [x, w_bf16]
    if rope:
        cos, sin = tables
        nt = cos.shape[0] // tm
        tab = pl.BlockSpec((tm, LANES), lambda i: (i % nt, 0))
        in_specs += [tab, tab]
        args += [cos, sin]
    f32_out = jax.ShapeDtypeStruct((m, d), F32)
    bf_out = jax.ShapeDtypeStruct((m, d), BF16)
    spec = pl.BlockSpec((tm, d), row)
    aliases = {}
    if decode:
        out_shape = [f32_out] * 3
        out_specs = [spec] * 3
    else:
        tiles_per_seq = seq // tm
        slab = jax.ShapeDtypeStruct((depth, m // seq, d, seq), F32)
        slab_spec = pl.BlockSpec((None, None, d, tm),
                                 lambda i: (layer, i // tiles_per_seq, 0, i % tiles_per_seq))
        n_q = 2 if moba else 1
        out_shape = [bf_out] * n_q + [slab, slab, bf_out, bf_out]
        out_specs = [spec] * n_q + [slab_spec, slab_spec, spec, spec]
        if kv_bufs is not None:
            aliases = {len(args): n_q, len(args) + 1: n_q + 1}
            in_specs += [pl.BlockSpec(memory_space=pl.ANY)] * 2
            args += list(kv_bufs)
        if moba:
            nb = tm // MOBA_BLOCK
            out_shape.append(jax.ShapeDtypeStruct((m // MOBA_BLOCK, 1, d), F32))
            out_specs.append(pl.BlockSpec((nb, 1, d), lambda i: (i, 0, 0)))
    scale = 1.0 / math.sqrt(head_dim)
    if moba and not decode:
        scale *= math.log2(math.e)
    kern = functools.partial(_qkv_kernel, rope=rope, moba=moba, decode=decode, aliased=bool(aliases),
                             d=d, scale=scale)
    return pl.pallas_call(
        kern, grid=grid, in_specs=in_specs, out_specs=out_specs, out_shape=out_shape,
        input_output_aliases=aliases,
        compiler_params=pltpu.CompilerParams(dimension_semantics=("arbitrary",),
                                             vmem_limit_bytes=VMEM_LIMIT),
        name="qkv_proj_decode" if decode else ("qkv_proj_moba" if moba else "qkv_proj_sb"),
    )(*args)


def _head_masks(tq):
    lane = lax.broadcasted_iota(jnp.int32, (tq, LANES), 1)
    return lane < (LANES // 2)


def _sb_prompt_kernel(q_ref, k_ref, v_ref, o_ref, *, tile):
    qi = pl.program_id(2)
    q = q_ref[...]
    lo_half = _head_masks(tile)
    row = lax.broadcasted_iota(jnp.int32, (tile, tile), 0)
    col = lax.broadcasted_iota(jnp.int32, (tile, tile), 1)
    causal = col < row
    suffix = (row >= col).astype(BF16)
    zero_q = jnp.zeros_like(q)
    qh = [jnp.where(lo_half, q, zero_q), jnp.where(lo_half, zero_q, q)]

    def load(t):
        start = pl.multiple_of(t * tile, tile)
        return k_ref[pl.ds(start, tile), :], v_ref[pl.ds(start, tile), :]

    def weights(z, lg, carry):
        lg_hi, lg_lo = _hi_lo(lg)
        rev = _dot(lg_hi, suffix) + _dot(lg_lo, suffix)
        return jnp.exp(z + rev + carry)

    k_d, v_d = load(qi)
    state = []
    for h in range(2):
        z = _dot_nt(qh[h], k_d)
        lg = jnp.where(causal, _log_one_minus_sigmoid(z), 0.0)
        a = jnp.where(causal, weights(z, lg, 0.0), 0.0)
        state += [jnp.sum(lg, axis=1, keepdims=True), _dot(a.astype(BF16), v_d)]

    def body(p, st):
        t_hi = qi - 1 - 2 * p
        t_lo = t_hi - 1
        k_hi, v_hi = load(t_hi)
        k_lo, v_lo = load(jnp.maximum(t_lo, 0))
        v_lo = jnp.where(t_lo >= 0, v_lo, jnp.zeros_like(v_lo))
        out = []
        for h in range(2):
            carry, acc = st[2 * h], st[2 * h + 1]
            z_hi = _dot_nt(qh[h], k_hi)
            z_lo = _dot_nt(qh[h], k_lo)
            lg_hi = _log_one_minus_sigmoid(z_hi)
            lg_lo = _log_one_minus_sigmoid(z_lo)
            sum_hi = jnp.sum(lg_hi, axis=1, keepdims=True)
            a_hi = weights(z_hi, lg_hi, carry)
            a_lo = weights(z_lo, lg_lo, carry + sum_hi)
            acc = acc + _dot(a_hi.astype(BF16), v_hi) + _dot(a_lo.astype(BF16), v_lo)
            carry = carry + sum_hi + jnp.sum(lg_lo, axis=1, keepdims=True)
            out += [carry, acc]
        return tuple(out)

    st = lax.fori_loop(0, (qi + 1) // 2, body, tuple(state))
    o_ref[...] = jnp.where(lo_half, st[1], st[3]).astype(o_ref.dtype)


def _moba_prompt_kernel(qhi_ref, qlo_ref, k_ref, v_ref, km_ref, o_ref, *, tile, n_blocks):
    qi = pl.program_id(2)
    q_hi = qhi_ref[...]
    q_lo = qlo_ref[...]
    lo_half = _head_masks(tile)
    row = lax.broadcasted_iota(jnp.int32, (tile, tile), 0)
    col = lax.broadcasted_iota(jnp.int32, (tile, tile), 1)
    causal = col <= row
    km_hi, km_lo = _hi_lo(km_ref[...])
    lane = lax.broadcasted_iota(jnp.int32, (tile, LANES), 1)
    zero_q = jnp.zeros_like(q_hi)
    band_start = (LANES // 2, 0)

    def load(n):
        start = pl.multiple_of(n * tile, tile)
        return k_ref[pl.ds(start, tile), :], v_ref[pl.ds(start, tile), :]

    def in_band_rows(x, start):
        parts = [jnp.zeros((start, LANES), x.dtype)] if start else []
        parts.append(x)
        rest = LANES - start - n_blocks
        if rest:
            parts.append(jnp.zeros((rest, LANES), x.dtype))
        return jnp.concatenate(parts, axis=0)

    k_own, v_own = load(qi)
    qh, qsel, bands, state = [], [], [], []
    for h in range(2):
        hm = lo_half if h == 0 else jnp.logical_not(lo_half)
        qh.append(jnp.where(hm, q_hi, zero_q))
        ql = jnp.where(hm, q_lo, zero_q)
        kmb_hi = in_band_rows(km_hi, band_start[h])
        kmb_lo = in_band_rows(km_lo, band_start[h])
        blk = lane - band_start[h]
        in_band = (blk >= 0) & (blk < n_blocks)
        gate = _dot_nt(qh[h], kmb_hi) + _dot_nt(qh[h], kmb_lo) + _dot_nt(ql, kmb_hi)
        gate = jnp.where(in_band & (blk < qi), gate, -jnp.inf)
        picked = jnp.zeros((tile, LANES), jnp.bool_)
        for r in range(MOBA_TOP_K):
            mx = jnp.max(gate, axis=1, keepdims=True)
            idx = jnp.min(jnp.where(gate == mx, blk, LANES), axis=1, keepdims=True)
            hit = blk == idx
            picked = picked | (hit & (r < qi))
            gate = jnp.where(hit, -jnp.inf, gate)
        bias = jnp.where(picked, 0.0, MASK_VALUE).astype(BF16)
        qsel.append(jnp.where(in_band, bias, qh[h]))
        bands.append(in_band)

    def keys(k, n, h):
        one, zero = jnp.ones_like(k), jnp.zeros_like(k)
        return jnp.where(lane == band_start[h] + n, one, jnp.where(bands[h], zero, k))

    def values(v, h):
        one = jnp.ones_like(v)
        return jnp.where(lo_half, v, one) if h == 0 else jnp.where(lo_half, one, v)

    for h in range(2):
        s = jnp.where(causal, _dot_nt(qh[h], k_own), MASK_VALUE)
        m = jnp.max(s, axis=1, keepdims=True)
        state += [m, _dot(jnp.exp2(s - m).astype(BF16), values(v_own, h))]

    nb = MOBA_BLOCKS_PER_TRIP

    def body(j, st):
        ns = [nb * j + i for i in range(nb)]
        kv = [load(n) for n in ns]
        out = []
        for h in range(2):
            m, acc = st[2 * h], st[2 * h + 1]
            ss = [_dot_nt(qsel[h], keys(k, n, h)) for n, (k, _) in zip(ns, kv)]
            m_new = jnp.maximum(m, jnp.max(functools.reduce(jnp.maximum, ss), axis=1, keepdims=True))
            alpha = jnp.exp2(m - m_new)
            pv = [_dot(jnp.exp2(s - m_new).astype(BF16), values(v, h)) for s, (_, v) in zip(ss, kv)]
            out += [m_new, alpha * acc + functools.reduce(jnp.add, pv)]
        return tuple(out)

    st = lax.fori_loop(0, (qi + nb - 1) // nb, body, tuple(state))
    num = jnp.where(lo_half, st[1], st[3])
    den = jnp.where(lo_half, pltpu.roll(st[1], LANES // 2, 1), pltpu.roll(st[3], LANES // 2, 1))
    o_ref[...] = (num / den).astype(o_ref.dtype)


def _prompt_attention(q_parts, kb, vb, kmean, *, batch, seq, moba):
    m, d = kb.shape
    tile = ATTN_TILE
    nq = seq // tile
    n_pairs = d // LANES
    grid = (batch, n_pairs, nq)
    q_spec = pl.BlockSpec((tile, LANES), lambda b, p, i: (b * nq + i, p))
    kv_spec = pl.BlockSpec((seq, LANES), lambda b, p, i: (b, p))
    in_specs = [q_spec] * len(q_parts) + [kv_spec, kv_spec]
    args = list(q_parts) + [kb, vb]
    if moba:
        n_blocks = seq // MOBA_BLOCK
        in_specs.append(pl.BlockSpec((None, n_blocks, LANES), lambda b, p, i: (b, 0, p)))
        args.append(kmean)
        kern = functools.partial(_moba_prompt_kernel, tile=tile, n_blocks=n_blocks)
    else:
        kern = functools.partial(_sb_prompt_kernel, tile=tile)
    return pl.pallas_call(
        kern, grid=grid, in_specs=in_specs, out_specs=q_spec,
        out_shape=jax.ShapeDtypeStruct((m, d), BF16),
        compiler_params=pltpu.CompilerParams(dimension_semantics=("arbitrary",) * 3,
                                             vmem_limit_bytes=VMEM_LIMIT),
        name="moba_prompt" if moba else "sb_prompt",
    )(*args)


def _decode_kernel(pt_ref, q_ref, kn_ref, vn_ref, *refs, pps, page, moba, n_steps):
    del pt_ref
    kt_refs = refs[:pps]
    vt_refs = refs[pps:2 * pps]
    o_ref = refs[2 * pps]
    scratch = refs[2 * pps + 1:]
    d = q_ref.shape[-1]
    hd = d // N_HEADS
    s = pl.program_id(1)
    heads = [slice(h * hd, (h + 1) * hd) for h in range(N_HEADS)]

    def column(row_ref):
        return jnp.broadcast_to(row_ref[0], (page, d)).T

    def scores(kt, qcol):
        return jnp.concatenate(
            [jnp.sum(kt[sl, :] * qcol[sl, :], axis=0, keepdims=True) for sl in heads], axis=0)

    def finish(out_t):
        o_ref[0] = jnp.sum(out_t.T, axis=0, keepdims=True)

    if not moba:
        qcol_ref, carry_ref, acc_ref = scratch

        @pl.when(s == 0)
        def _():
            qcol_ref[...] = column(q_ref)
            carry_ref[...] = jnp.zeros_like(carry_ref)
            acc_ref[...] = jnp.zeros_like(acc_ref)

        r = lax.broadcasted_iota(jnp.int32, (page, page), 0)
        c = lax.broadcasted_iota(jnp.int32, (page, page), 1)
        suffix = (r >= c).astype(BF16)
        carry = carry_ref[...]
        for j in reversed(range(pps)):
            z = scores(kt_refs[j], qcol_ref)
            lg = _log_one_minus_sigmoid(z)
            lg_hi, lg_lo = _hi_lo(lg)
            rev = _dot(lg_hi, suffix) + _dot(lg_lo, suffix)
            a = jnp.exp(z + rev + carry)
            for h, sl in enumerate(heads):
                acc_ref[sl, :] += vt_refs[j][sl, :] * a[h:h + 1, :]
            carry = carry + jnp.sum(lg, axis=1, keepdims=True)
        carry_ref[...] = carry

        @pl.when(s == n_steps - 1)
        def _():
            finish(acc_ref[...])
        return

    qcol_ref, m_ref, l_ref, g_ref, acc_ref = scratch
    pages_per_block = MOBA_BLOCK // page
    n_past = n_steps * pps // pages_per_block

    @pl.when(s == 0)
    def _():
        qcol_ref[...] = column(q_ref)

    for jb in range(pps // pages_per_block):
        n = s * (pps // pages_per_block) + jb
        ps = [jb * pages_per_block + i for i in range(pages_per_block)]
        zs = [scores(kt_refs[p], qcol_ref) for p in ps]
        m_n = functools.reduce(jnp.maximum, [jnp.max(z, axis=1, keepdims=True) for z in zs])
        es = [jnp.exp(z - m_n) for z in zs]
        l_n = functools.reduce(jnp.add, [jnp.sum(e, axis=1, keepdims=True) for e in es])
        g_n = functools.reduce(jnp.add, [jnp.sum(z, axis=1, keepdims=True) for z in zs]) * (1.0 / MOBA_BLOCK)
        m_ref[n] = jnp.broadcast_to(m_n, (N_HEADS, page))
        l_ref[n] = jnp.broadcast_to(l_n, (N_HEADS, page))
        g_ref[n] = jnp.broadcast_to(g_n, (N_HEADS, page))
        for h, sl in enumerate(heads):
            acc_ref[n, sl, :] = functools.reduce(
                jnp.add, [vt_refs[p][sl, :] * e[h:h + 1, :] for p, e in zip(ps, es)])

    @pl.when(s == n_steps - 1)
    def _():
        g = [g_ref[n] for n in range(n_past)]
        m = [m_ref[n] for n in range(n_past)]
        chosen = []
        for n in range(n_past):
            rank = jnp.zeros((N_HEADS, page), jnp.int32)
            for o in range(n_past):
                if o != n:
                    ahead = (g[o] >= g[n]) if o < n else (g[o] > g[n])
                    rank = rank + ahead.astype(jnp.int32)
            chosen.append(rank < MOBA_TOP_K)
        s_own = scores(column(kn_ref), qcol_ref)
        m_all = functools.reduce(jnp.maximum, [jnp.where(ch, mm, MASK_VALUE) for ch, mm in zip(chosen, m)] + [s_own])
        w = [jnp.where(ch, jnp.exp(mm - m_all), 0.0) for ch, mm in zip(chosen, m)]
        w_own = jnp.exp(s_own - m_all)
        denom = functools.reduce(jnp.add, [wn * l_ref[n] for n, wn in enumerate(w)]) + w_own
        inv = 1.0 / denom
        lane = lax.broadcasted_iota(jnp.int32, (N_HEADS, page), 1)
        w_own = jnp.where(lane == 0, w_own * inv, 0.0)
        w = [wn * inv for wn in w]
        vcol = column(vn_ref)
        parts = []
        for h, sl in enumerate(heads):
            part = vcol[sl, :] * w_own[h:h + 1, :]
            for n in range(n_past):
                part = part + acc_ref[n, sl, :] * w[n][h:h + 1, :]
            parts.append(part)
        finish(jnp.concatenate(parts, axis=0))


def _decode_attention(layer, q, k_new, v_new, cache_kt, cache_vt, page_table, *, moba):
    db, _, d = q.shape
    n_pages = page_table.shape[1]
    page = cache_kt.shape[3]
    pps = DECODE_PAGES_PER_STEP
    n_steps = n_pages // pps
    vec = pl.BlockSpec((1, 1, d), lambda b, s, pt: (b, 0, 0))

    def page_spec(j):
        if moba:
            idx = lambda b, s, pt: (layer, pt[b, s * pps + j], 0, 0)
        else:
            idx = lambda b, s, pt: (layer, pt[b, (n_steps - 1 - s) * pps + j], 0, 0)
        return pl.BlockSpec((None, None, d, page), idx)

    in_specs = [vec, vec, vec] + [page_spec(j) for j in range(pps)] * 2
    lanes_eq = pltpu.VMEM((N_HEADS, page), F32)
    if moba:
        n_past = n_pages * page // MOBA_BLOCK
        stats = pltpu.VMEM((n_past, N_HEADS, page), F32)
        scratch = [pltpu.VMEM((d, page), F32), stats, stats, stats, pltpu.VMEM((n_past, d, page), F32)]
    else:
        scratch = [pltpu.VMEM((d, page), F32), lanes_eq, pltpu.VMEM((d, page), F32)]
    kern = functools.partial(_decode_kernel, pps=pps, page=page, moba=moba, n_steps=n_steps)
    return pl.pallas_call(
        kern,
        grid_spec=pltpu.PrefetchScalarGridSpec(
            num_scalar_prefetch=1, grid=(db, n_steps), in_specs=in_specs, out_specs=vec,
            scratch_shapes=scratch),
        out_shape=jax.ShapeDtypeStruct((db, 1, d), F32),
        compiler_params=pltpu.CompilerParams(dimension_semantics=("arbitrary", "arbitrary"),
                                             vmem_limit_bytes=VMEM_LIMIT),
        name="moba_decode" if moba else "sb_decode",
    )(page_table, q, k_new, v_new, *([cache_kt] * pps), *([cache_vt] * pps))


def _gelu_tanh(x):
    return 0.5 * x * (1.0 + jnp.tanh(math.sqrt(2.0 / math.pi) * (x + 0.044715 * (x * x * x))))


def _post_kernel(*refs, decode, alpha, d_ff, tiles_per_seq):
    if decode:
        (x_ref, o_ref, wo_ref, g1_ref, b1_ref, win_ref, cw_ref, cb_ref, wd_ref, g2_ref, b2_ref,
         st_ref, y_ref, cs_ref, h_ref) = refs
    else:
        (x_ref, o_ref, wo_ref, g1_ref, b1_ref, win_ref, cw_ref, cb_ref, wd_ref, g2_ref, b2_ref,
         y_ref, cs_ref, h_ref, tail_ref) = refs
        i = pl.program_id(0)

        @pl.when(i % tiles_per_seq == 0)
        def _():
            tail_ref[...] = jnp.zeros_like(tail_ref)

    x = x_ref[...]
    tm = x.shape[0]
    attn = _dot(o_ref[...].astype(BF16), wo_ref[...])
    x1 = _layer_norm(alpha * x + attn, g1_ref[...], b1_ref[...])
    x1b = x1.astype(BF16)
    for c in range(d_ff // FF_CHUNK):
        lo, hi = c * FF_CHUNK, (c + 1) * FF_CHUNK
        a = _dot(x1b, win_ref[:, lo:hi])
        g = _dot(x1b, win_ref[:, d_ff + lo:d_ff + hi])
        if decode:
            prev2 = st_ref[:, lo:hi]
            prev1 = st_ref[:, d_ff + lo:d_ff + hi]
            cs_ref[:, lo:hi] = prev1
            cs_ref[:, d_ff + lo:d_ff + hi] = a
        else:
            ext = jnp.concatenate([tail_ref[:, lo:hi], a], axis=0)
            prev1 = ext[SUBLANES - 1:SUBLANES - 1 + tm]
            prev2 = ext[SUBLANES - 2:SUBLANES - 2 + tm]
            last = a[tm - SUBLANES:]
            tail_ref[:, lo:hi] = last
            cs_ref[:, lo:hi] = last
        conv = cb_ref[:, lo:hi] + cw_ref[0:1, lo:hi] * prev2 + cw_ref[1:2, lo:hi] * prev1 + cw_ref[2:3, lo:hi] * a
        h_ref[:, lo:hi] = (_gelu_tanh(conv) * g).astype(BF16)
    f = _dot(h_ref[...], wd_ref[...])
    y_ref[...] = _layer_norm(alpha * x1 + f, g2_ref[...], b2_ref[...])


def _post_attention(x, o, w_o, ln1_g, ln1_b, w_in, conv_w, conv_b, w_down, ln2_g, ln2_b, state,
                    *, alpha, tm, seq_rows):
    m, d = x.shape
    d_ff = w_down.shape[0]
    decode = state is not None
    grid = (m // tm,)
    row = lambda i: (i, 0)
    const = lambda i: (0, 0)

    def resident(shape):
        return pl.BlockSpec(shape, const, pipeline_mode=pl.Buffered(1))

    in_specs = [pl.BlockSpec((tm, d), row), pl.BlockSpec((tm, d), row),
                resident((d, d)), resident((1, d)), resident((1, d)),
                resident((d, 2 * d_ff)), resident((CONV_W, d_ff)), resident((1, d_ff)),
                resident((d_ff, d)), resident((1, d)), resident((1, d))]
    args = [x, o, w_o, ln1_g, ln1_b, w_in, conv_w, conv_b, w_down, ln2_g, ln2_b]
    scratch = [pltpu.VMEM((tm, d_ff), BF16)]
    if decode:
        in_specs.append(pl.BlockSpec((tm, 2 * d_ff), row))
        args.append(state)
        cs_shape = jax.ShapeDtypeStruct((m, 2 * d_ff), F32)
        cs_spec = pl.BlockSpec((tm, 2 * d_ff), row)
    else:
        tiles_per_seq = seq_rows // tm
        cs_shape = jax.ShapeDtypeStruct((m // seq_rows, SUBLANES, d_ff), F32)
        cs_spec = pl.BlockSpec((None, SUBLANES, d_ff), lambda i: (i // tiles_per_seq, 0, 0))
        scratch.append(pltpu.VMEM((SUBLANES, d_ff), F32))
    kern = functools.partial(_post_kernel, decode=decode, alpha=alpha, d_ff=d_ff,
                             tiles_per_seq=None if decode else seq_rows // tm)
    return pl.pallas_call(
        kern, grid=grid, in_specs=in_specs,
        out_specs=[pl.BlockSpec((tm, d), row), cs_spec],
        out_shape=[jax.ShapeDtypeStruct((m, d), F32), cs_shape],
        scratch_shapes=scratch,
        compiler_params=pltpu.CompilerParams(dimension_semantics=("arbitrary",),
                                             vmem_limit_bytes=VMEM_LIMIT),
        name="post_attn_ffn_decode" if decode else "post_attn_ffn",
    )(*args)


def kernel(x_prompt, x_sample, cache_k, cache_v, state_conv, page_table, w_qkv, w_o, ln1_g, ln1_b,
           w_in, conv_w, conv_b, w_down, ln2_g, ln2_b):
    batch, seq, d = x_prompt.shape
    db, dec_seq, _ = x_sample.shape
    depth, n_pool, page = cache_k.shape[:3]
    n_pages = page_table.shape[1]
    past_len = n_pages * page
    d_ff = w_down.shape[1]
    head_dim = d // N_HEADS
    assert dec_seq == 1 and 2 * head_dim == LANES and d_ff % FF_CHUNK == 0 and page == LANES
    assert seq % MOBA_BLOCK == 0 and seq % ROW_TILE == 0 and ROW_TILE % MOBA_BLOCK == 0
    assert past_len % MOBA_BLOCK == 0 and MOBA_BLOCK % page == 0 and db % SUBLANES == 0
    assert (seq // MOBA_BLOCK) % MOBA_BLOCKS_PER_TRIP == 0 and seq // MOBA_BLOCK <= LANES // 2
    assert n_pages % DECODE_PAGES_PER_STEP == 0 and (DECODE_PAGES_PER_STEP * page) % MOBA_BLOCK == 0
    alpha = (2.0 * depth) ** 0.25

    xp = x_prompt.reshape(batch * seq, d)
    xs = x_sample.reshape(db, d)
    ckt = jnp.transpose(cache_k, (0, 1, 3, 4, 2)).reshape(depth, n_pool, d, page)
    cvt = jnp.transpose(cache_v, (0, 1, 3, 4, 2)).reshape(depth, n_pool, d, page)
    st = state_conv.reshape(depth, db, (CONV_W - 1) * d_ff)
    tables_p = _rope_tables(jnp.arange(seq, dtype=jnp.int32), head_dim)
    tables_s = _rope_tables(jnp.full((db,), past_len, jnp.int32), head_dim)
    row2 = lambda a: a.reshape(1, -1)

    cp_l, ks_l, vs_l, cs_l = [], [], [], []
    kv_bufs = None
    for l in range(depth):
        moba = l % N_MIXERS == 1
        wq = w_qkv[l].astype(BF16)
        post_w = (w_o[l].astype(BF16), row2(ln1_g[l]), row2(ln1_b[l]), w_in[l].astype(BF16), conv_w[l],
                  row2(conv_b[l]), w_down[l].astype(BF16), row2(ln2_g[l]), row2(ln2_b[l]))
        outs = _qkv_proj(xp, wq, tables_p, kv_bufs, rope=moba, moba=moba, decode=False, tm=ROW_TILE,
                         layer=l, depth=depth, seq=seq)
        if moba:
            q_hi, q_lo, kt, vt, kb, vb, km = outs
            o = _prompt_attention((q_hi, q_lo), kb, vb, km.reshape(batch, seq // MOBA_BLOCK, d),
                                  batch=batch, seq=seq, moba=True)
        else:
            q_hi, kt, vt, kb, vb = outs
            o = _prompt_attention((q_hi,), kb, vb, None, batch=batch, seq=seq, moba=False)
        kv_bufs = (kt, vt)
        xp, tail = _post_attention(xp, o, *post_w, None, alpha=alpha, tm=ROW_TILE, seq_rows=seq)
        cp_l.append(tail[:, SUBLANES - (CONV_W - 1):, :])
        qs, ksn, vsn = _qkv_proj(xs, wq, tables_s, rope=moba, moba=moba, decode=True, tm=db)
        os_ = _decode_attention(l, qs.reshape(db, 1, d), ksn.reshape(db, 1, d), vsn.reshape(db, 1, d),
                                ckt, cvt, page_table, moba=moba)
        xs, cs = _post_attention(xs, os_.reshape(db, d), *post_w, st[l], alpha=alpha, tm=db, seq_rows=1)
        ks_l.append(ksn.reshape(db, 1, N_HEADS, head_dim))
        vs_l.append(vsn.reshape(db, 1, N_HEADS, head_dim))
        cs_l.append(cs.reshape(db, CONV_W - 1, d_ff))
    new_k, new_v = (jnp.transpose(t.reshape(depth, batch, N_HEADS, head_dim, seq), (0, 1, 4, 2, 3))
                    for t in kv_bufs)
    return (xp.reshape(batch, seq, d), xs.reshape(db, 1, d), new_k, new_v,
            jnp.stack(cp_l), jnp.stack(ks_l), jnp.stack(vs_l), jnp.stack(cs_l))
```

```python
import functools
import math

import jax
import jax.numpy as jnp
from jax import lax
from jax.experimental import pallas as pl
from jax.experimental.pallas import tpu as pltpu

N_HEADS = 16
MOBA_BLOCK = 256
MOBA_TOP_K = 3
ROPE_THETA = 10000.0
LN_EPS = 1e-5
N_MIXERS = 2
CONV_W = 3

LANES = 128
SUBLANES = 8
ROW_TILE = 512
FF_CHUNK = 256
ATTN_TILE = MOBA_BLOCK
MOBA_BLOCKS_PER_TRIP = 4
DECODE_PAGES_PER_STEP = 4
VMEM_LIMIT = 56 * 1024 * 1024
MASK_VALUE = -1e30
SB_DEAD_CARRY = -104.0

F32 = jnp.float32
BF16 = jnp.bfloat16


def _dot(a, b):
    return jnp.dot(a, b, preferred_element_type=F32)


def _dot_nt(a, b):
    return lax.dot_general(a, b, (((1,), (1,)), ((), ())), preferred_element_type=F32)


def _layer_norm(y, g, b):
    mu = jnp.mean(y, axis=-1, keepdims=True)
    d = y - mu
    var = jnp.mean(d * d, axis=-1, keepdims=True)
    return d * lax.rsqrt(var + LN_EPS) * g + b


def _log_one_minus_sigmoid(z):
    return -(jnp.maximum(z, 0.0) + jnp.log(1.0 + jnp.exp(-jnp.abs(z))))


def _hi_lo(x):
    hi = x.astype(BF16)
    return hi, (x - hi.astype(F32)).astype(BF16)


def _rope(x, cos, sin_signed):
    lane = lax.broadcasted_iota(jnp.int32, cos.shape, 1)
    first_half = (lane % 64) < 32
    cols = []
    for c in range(x.shape[1] // LANES):
        xc = x[:, c * LANES:(c + 1) * LANES]
        partner = jnp.where(first_half, pltpu.roll(xc, 96, 1), pltpu.roll(xc, 32, 1))
        cols.append(xc * cos + partner * sin_signed)
    return jnp.concatenate(cols, axis=1)


def _rope_tables(pos, head_dim):
    half = head_dim // 2
    inv = ROPE_THETA ** (-jnp.arange(half, dtype=F32) / half)
    ang = pos.astype(F32)[:, None] * inv[None, :]
    cos, sin = jnp.cos(ang), jnp.sin(ang)
    reps = LANES // head_dim
    return jnp.tile(jnp.concatenate([cos, cos], -1), (1, reps)), jnp.tile(jnp.concatenate([-sin, sin], -1), (1, reps))


def _qkv_kernel(*refs, rope, moba, decode, aliased, d, scale):
    n_in = 2 + (2 if rope else 0) + (2 if aliased else 0)
    x_ref, w_ref = refs[:2]
    outs = refs[n_in:]
    xb = x_ref[...].astype(BF16)
    q = _dot(xb, w_ref[:, 0:d])
    k = _dot(xb, w_ref[:, d:2 * d])
    v = _dot(xb, w_ref[:, 2 * d:3 * d])
    if rope:
        cos, sin = refs[2][...], refs[3][...]
        q = _rope(q, cos, sin)
        k = _rope(k, cos, sin)
    q = q * scale
    if decode:
        q_ref, k_ref, v_ref = outs
        q_ref[...] = q
        k_ref[...] = k
        v_ref[...] = v
        return
    if moba:
        qhi_ref, qlo_ref, kt_ref, vt_ref, kb_ref, vb_ref, km_ref = outs
    else:
        qhi_ref, kt_ref, vt_ref, kb_ref, vb_ref = outs
    q_hi = q.astype(BF16)
    qhi_ref[...] = q_hi
    kt_ref[...] = k.T
    vt_ref[...] = v.T
    kb_ref[...] = k.astype(BF16)
    vb_ref[...] = v.astype(BF16)
    if moba:
        qlo_ref[...] = (q - q_hi.astype(F32)).astype(BF16)
        for j in range(k.shape[0] // MOBA_BLOCK):
            blk = k[j * MOBA_BLOCK:(j + 1) * MOBA_BLOCK]
            km_ref[j] = jnp.sum(blk, axis=0, keepdims=True) * (1.0 / MOBA_BLOCK)


def _qkv_proj(x, w_bf16, tables, kv_bufs=None, *, rope, moba, decode, tm, layer=0, depth=1, seq=None):
    m, d = x.shape
    head_dim = d // N_HEADS
    grid = (m // tm,)
    row = lambda i: (i, 0)
    in_specs = [pl.BlockSpec((tm, d), row),
                pl.BlockSpec((d, 3 * d), lambda i: (0, 0), pipeline_mode=pl.Buffered(1))]
    args = [x, w_bf16]
    if rope:
        cos, sin = tables
        nt = cos.shape[0] // tm
        tab = pl.BlockSpec((tm, LANES), lambda i: (i % nt, 0))
        in_specs += [tab, tab]
        args += [cos, sin]
    f32_out = jax.ShapeDtypeStruct((m, d), F32)
    bf_out = jax.ShapeDtypeStruct((m, d), BF16)
    spec = pl.BlockSpec((tm, d), row)
    aliases = {}
    if decode:
        out_shape = [f32_out] * 3
        out_specs = [spec] * 3
    else:
        tiles_per_seq = seq // tm
        slab = jax.ShapeDtypeStruct((depth, m // seq, d, seq), F32)
        slab_spec = pl.BlockSpec((None, None, d, tm),
                                 lambda i: (layer, i // tiles_per_seq, 0, i % tiles_per_seq))
        n_q = 2 if moba else 1
        out_shape = [bf_out] * n_q + [slab, slab, bf_out, bf_out]
        out_specs = [spec] * n_q + [slab_spec, slab_spec, spec, spec]
        if kv_bufs is not None:
            aliases = {len(args): n_q, len(args) + 1: n_q + 1}
            in_specs += [pl.BlockSpec(memory_space=pl.ANY)] * 2
            args += list(kv_bufs)
        if moba:
            nb = tm // MOBA_BLOCK
            out_shape.append(jax.ShapeDtypeStruct((m // MOBA_BLOCK, 1, d), F32))
            out_specs.append(pl.BlockSpec((nb, 1, d), lambda i: (i, 0, 0)))
    scale = 1.0 / math.sqrt(head_dim)
    if moba and not decode:
        scale *= math.log2(math.e)
    kern = functools.partial(_qkv_kernel, rope=rope, moba=moba, decode=decode, aliased=bool(aliases),
                             d=d, scale=scale)
    return pl.pallas_call(
        kern, grid=grid, in_specs=in_specs, out_specs=out_specs, out_shape=out_shape,
        input_output_aliases=aliases,
        compiler_params=pltpu.CompilerParams(dimension_semantics=("arbitrary",),
                                             vmem_limit_bytes=VMEM_LIMIT),
        name="qkv_proj_decode" if decode else ("qkv_proj_moba" if moba else "qkv_proj_sb"),
    )(*args)


def _head_masks(tq):
    lane = lax.broadcasted_iota(jnp.int32, (tq, LANES), 1)
    return lane < (LANES // 2)


def _sb_prompt_kernel(q_ref, k_ref, v_ref, o_ref, *, tile):
    qi = pl.program_id(2)
    q = q_ref[...]
    lo_half = _head_masks(tile)
    row = lax.broadcasted_iota(jnp.int32, (tile, tile), 0)
    col = lax.broadcasted_iota(jnp.int32, (tile, tile), 1)
    causal = col < row
    suffix = (row >= col).astype(BF16)
    zero_q = jnp.zeros_like(q)
    qh = [jnp.where(lo_half, q, zero_q), jnp.where(lo_half, zero_q, q)]

    def load(t):
        start = pl.multiple_of(t * tile, tile)
        return k_ref[pl.ds(start, tile), :], v_ref[pl.ds(start, tile), :]

    def weights(z, lg, carry):
        lg_hi, lg_lo = _hi_lo(lg)
        rev = _dot(lg_hi, suffix) + _dot(lg_lo, suffix)
        return jnp.exp(z + rev + carry)

    k_d, v_d = load(qi)
    state = []
    for h in range(2):
        z = _dot_nt(qh[h], k_d)
        lg = jnp.where(causal, _log_one_minus_sigmoid(z), 0.0)
        a = jnp.where(causal, weights(z, lg, 0.0), 0.0)
        state += [jnp.sum(lg, axis=1, keepdims=True), _dot(a.astype(BF16), v_d)]

    def body(p, st):
        t_hi = qi - 1 - 2 * p
        t_lo = t_hi - 1
        k_hi, v_hi = load(t_hi)
        k_lo, v_lo = load(jnp.maximum(t_lo, 0))
        v_lo = jnp.where(t_lo >= 0, v_lo, jnp.zeros_like(v_lo))
        out = []
        for h in range(2):
            carry, acc = st[2 * h], st[2 * h + 1]
            z_hi = _dot_nt(qh[h], k_hi)
            z_lo = _dot_nt(qh[h], k_lo)
            lg_hi = _log_one_minus_sigmoid(z_hi)
            lg_lo = _log_one_minus_sigmoid(z_lo)
            sum_hi = jnp.sum(lg_hi, axis=1, keepdims=True)
            a_hi = weights(z_hi, lg_hi, carry)
            a_lo = weights(z_lo, lg_lo, carry + sum_hi)
            acc = acc + _dot(a_hi.astype(BF16), v_hi) + _dot(a_lo.astype(BF16), v_lo)
            carry = carry + sum_hi + jnp.sum(lg_lo, axis=1, keepdims=True)
            out += [carry, acc]
        return tuple(out)

    def alive(st):
        return jnp.maximum(jnp.max(st[0]), jnp.max(st[2])) > SB_DEAD_CARRY

    n_trips = (qi + 1) // 2

    def step(c):
        st = body(c[0], c[2:])
        return (c[0] + 1, alive(st)) + st

    c = lax.while_loop(lambda c: (c[0] < n_trips) & c[1], step,
                       (jnp.int32(0), alive(state)) + tuple(state))
    o_ref[...] = jnp.where(lo_half, c[3], c[5]).astype(o_ref.dtype)


def _moba_prompt_kernel(qhi_ref, qlo_ref, k_ref, v_ref, km_ref, o_ref, *, tile, n_blocks):
    qi = pl.program_id(2)
    q_hi = qhi_ref[...]
    q_lo = qlo_ref[...]
    lo_half = _head_masks(tile)
    row = lax.broadcasted_iota(jnp.int32, (tile, tile), 0)
    col = lax.broadcasted_iota(jnp.int32, (tile, tile), 1)
    causal = col <= row
    km_hi, km_lo = _hi_lo(km_ref[...])
    lane = lax.broadcasted_iota(jnp.int32, (tile, LANES), 1)
    zero_q = jnp.zeros_like(q_hi)
    band_start = (LANES // 2, 0)

    def load(n):
        start = pl.multiple_of(n * tile, tile)
        return k_ref[pl.ds(start, tile), :], v_ref[pl.ds(start, tile), :]

    def in_band_rows(x, start):
        parts = [jnp.zeros((start, LANES), x.dtype)] if start else []
        parts.append(x)
        rest = LANES - start - n_blocks
        if rest:
            parts.append(jnp.zeros((rest, LANES), x.dtype))
        return jnp.concatenate(parts, axis=0)

    k_own, v_own = load(qi)
    qh, qsel, bands, state = [], [], [], []
    for h in range(2):
        hm = lo_half if h == 0 else jnp.logical_not(lo_half)
        qh.append(jnp.where(hm, q_hi, zero_q))
        ql = jnp.where(hm, q_lo, zero_q)
        kmb_hi = in_band_rows(km_hi, band_start[h])
        kmb_lo = in_band_rows(km_lo, band_start[h])
        blk = lane - band_start[h]
        in_band = (blk >= 0) & (blk < n_blocks)
        gate = _dot_nt(qh[h], kmb_hi) + _dot_nt(qh[h], kmb_lo) + _dot_nt(ql, kmb_hi)
        gate = jnp.where(in_band & (blk < qi), gate, -jnp.inf)
        picked = jnp.zeros((tile, LANES), jnp.bool_)
        for r in range(MOBA_TOP_K):
            mx = jnp.max(gate, axis=1, keepdims=True)
            idx = jnp.min(jnp.where(gate == mx, blk, LANES), axis=1, keepdims=True)
            hit = blk == idx
            picked = picked | (hit & (r < qi))
            gate = jnp.where(hit, -jnp.inf, gate)
        bias = jnp.where(picked, 0.0, MASK_VALUE).astype(BF16)
        qsel.append(jnp.where(in_band, bias, qh[h]))
        bands.append(in_band)

    def keys(k, n, h):
        one, zero = jnp.ones_like(k), jnp.zeros_like(k)
        return jnp.where(lane == band_start[h] + n, one, jnp.where(bands[h], zero, k))

    def values(v, h):
        one = jnp.ones_like(v)
        return jnp.where(lo_half, v, one) if h == 0 else jnp.where(lo_half, one, v)

    for h in range(2):
        s = jnp.where(causal, _dot_nt(qh[h], k_own), MASK_VALUE)
        m = jnp.max(s, axis=1, keepdims=True)
        state += [m, _dot(jnp.exp2(s - m).astype(BF16), values(v_own, h))]

    nb = MOBA_BLOCKS_PER_TRIP

    def body(j, st):
        ns = [nb * j + i for i in range(nb)]
        kv = [load(n) for n in ns]
        out = []
        for h in range(2):
            m, acc = st[2 * h], st[2 * h + 1]
            ss = [_dot_nt(qsel[h], keys(k, n, h)) for n, (k, _) in zip(ns, kv)]
            m_new = jnp.maximum(m, jnp.max(functools.reduce(jnp.maximum, ss), axis=1, keepdims=True))
            alpha = jnp.exp2(m - m_new)
            pv = [_dot(jnp.exp2(s - m_new).astype(BF16), values(v, h)) for s, (_, v) in zip(ss, kv)]
            out += [m_new, alpha * acc + functools.reduce(jnp.add, pv)]
        return tuple(out)

    st = lax.fori_loop(0, (qi + nb - 1) // nb, body, tuple(state))
    num = jnp.where(lo_half, st[1], st[3])
    den = jnp.where(lo_half, pltpu.roll(st[1], LANES // 2, 1), pltpu.roll(st[3], LANES // 2, 1))
    o_ref[...] = (num / den).astype(o_ref.dtype)


def _prompt_attention(q_parts, kb, vb, kmean, *, batch, seq, moba):
    m, d = kb.shape
    tile = ATTN_TILE
    nq = seq // tile
    n_pairs = d // LANES
    grid = (batch, n_pairs, nq)
    q_spec = pl.BlockSpec((tile, LANES), lambda b, p, i: (b * nq + i, p))
    kv_spec = pl.BlockSpec((seq, LANES), lambda b, p, i: (b, p))
    in_specs = [q_spec] * len(q_parts) + [kv_spec, kv_spec]
    args = list(q_parts) + [kb, vb]
    if moba:
        n_blocks = seq // MOBA_BLOCK
        in_specs.append(pl.BlockSpec((None, n_blocks, LANES), lambda b, p, i: (b, 0, p)))
        args.append(kmean)
        kern = functools.partial(_moba_prompt_kernel, tile=tile, n_blocks=n_blocks)
    else:
        kern = functools.partial(_sb_prompt_kernel, tile=tile)
    return pl.pallas_call(
        kern, grid=grid, in_specs=in_specs, out_specs=q_spec,
        out_shape=jax.ShapeDtypeStruct((m, d), BF16),
        compiler_params=pltpu.CompilerParams(dimension_semantics=("arbitrary",) * 3,
                                             vmem_limit_bytes=VMEM_LIMIT),
        name="moba_prompt" if moba else "sb_prompt",
    )(*args)


def _decode_kernel(pt_ref, q_ref, kn_ref, vn_ref, *refs, pps, page, moba, n_steps):
    del pt_ref
    kt_refs = refs[:pps]
    vt_refs = refs[pps:2 * pps]
    o_ref = refs[2 * pps]
    scratch = refs[2 * pps + 1:]
    d = q_ref.shape[-1]
    hd = d // N_HEADS
    s = pl.program_id(1)
    heads = [slice(h * hd, (h + 1) * hd) for h in range(N_HEADS)]

    def column(row_ref):
        return jnp.broadcast_to(row_ref[0], (page, d)).T

    def scores(kt, qcol):
        return jnp.concatenate(
            [jnp.sum(kt[sl, :] * qcol[sl, :], axis=0, keepdims=True) for sl in heads], axis=0)

    def finish(out_t):
        o_ref[0] = jnp.sum(out_t.T, axis=0, keepdims=True)

    if not moba:
        qcol_ref, carry_ref, acc_ref = scratch

        @pl.when(s == 0)
        def _():
            qcol_ref[...] = column(q_ref)
            carry_ref[...] = jnp.zeros_like(carry_ref)
            acc_ref[...] = jnp.zeros_like(acc_ref)

        r = lax.broadcasted_iota(jnp.int32, (page, page), 0)
        c = lax.broadcasted_iota(jnp.int32, (page, page), 1)
        suffix = (r >= c).astype(BF16)
        carry = carry_ref[...]
        for j in reversed(range(pps)):
            z = scores(kt_refs[j], qcol_ref)
            lg = _log_one_minus_sigmoid(z)
            lg_hi, lg_lo = _hi_lo(lg)
            rev = _dot(lg_hi, suffix) + _dot(lg_lo, suffix)
            a = jnp.exp(z + rev + carry)
            for h, sl in enumerate(heads):
                acc_ref[sl, :] += vt_refs[j][sl, :] * a[h:h + 1, :]
            carry = carry + jnp.sum(lg, axis=1, keepdims=True)
        carry_ref[...] = carry

        @pl.when(s == n_steps - 1)
        def _():
            finish(acc_ref[...])
        return

    qcol_ref, m_ref, l_ref, g_ref, acc_ref = scratch
    pages_per_block = MOBA_BLOCK // page
    n_past = n_steps * pps // pages_per_block

    @pl.when(s == 0)
    def _():
        qcol_ref[...] = column(q_ref)

    for jb in range(pps // pages_per_block):
        n = s * (pps // pages_per_block) + jb
        ps = [jb * pages_per_block + i for i in range(pages_per_block)]
        zs = [scores(kt_refs[p], qcol_ref) for p in ps]
        m_n = functools.reduce(jnp.maximum, [jnp.max(z, axis=1, keepdims=True) for z in zs])
        es = [jnp.exp(z - m_n) for z in zs]
        l_n = functools.reduce(jnp.add, [jnp.sum(e, axis=1, keepdims=True) for e in es])
        g_n = functools.reduce(jnp.add, [jnp.sum(z, axis=1, keepdims=True) for z in zs]) * (1.0 / MOBA_BLOCK)
        m_ref[n] = jnp.broadcast_to(m_n, (N_HEADS, page))
        l_ref[n] = jnp.broadcast_to(l_n, (N_HEADS, page))
        g_ref[n] = jnp.broadcast_to(g_n, (N_HEADS, page))
        for h, sl in enumerate(heads):
            acc_ref[n, sl, :] = functools.reduce(
                jnp.add, [vt_refs[p][sl, :] * e[h:h + 1, :] for p, e in zip(ps, es)])

    @pl.when(s == n_steps - 1)
    def _():
        g = [g_ref[n] for n in range(n_past)]
        m = [m_ref[n] for n in range(n_past)]
        chosen = []
        for n in range(n_past):
            rank = jnp.zeros((N_HEADS, page), jnp.int32)
            for o in range(n_past):
                if o != n:
                    ahead = (g[o] >= g[n]) if o < n else (g[o] > g[n])
                    rank = rank + ahead.astype(jnp.int32)
            chosen.append(rank < MOBA_TOP_K)
        s_own = scores(column(kn_ref), qcol_ref)
        m_all = functools.reduce(jnp.maximum, [jnp.where(ch, mm, MASK_VALUE) for ch, mm in zip(chosen, m)] + [s_own])
        w = [jnp.where(ch, jnp.exp(mm - m_all), 0.0) for ch, mm in zip(chosen, m)]
        w_own = jnp.exp(s_own - m_all)
        denom = functools.reduce(jnp.add, [wn * l_ref[n] for n, wn in enumerate(w)]) + w_own
        inv = 1.0 / denom
        lane = lax.broadcasted_iota(jnp.int32, (N_HEADS, page), 1)
        w_own = jnp.where(lane == 0, w_own * inv, 0.0)
        w = [wn * inv for wn in w]
        vcol = column(vn_ref)
        parts = []
        for h, sl in enumerate(heads):
            part = vcol[sl, :] * w_own[h:h + 1, :]
            for n in range(n_past):
                part = part + acc_ref[n, sl, :] * w[n][h:h + 1, :]
            parts.append(part)
        finish(jnp.concatenate(parts, axis=0))


def _decode_attention(layer, q, k_new, v_new, cache_kt, cache_vt, page_table, *, moba):
    db, _, d = q.shape
    n_pages = page_table.shape[1]
    page = cache_kt.shape[3]
    pps = DECODE_PAGES_PER_STEP
    n_steps = n_pages // pps
    vec = pl.BlockSpec((1, 1, d), lambda b, s, pt: (b, 0, 0))

    def page_spec(j):
        if moba:
            idx = lambda b, s, pt: (layer, pt[b, s * pps + j], 0, 0)
        else:
            idx = lambda b, s, pt: (layer, pt[b, (n_steps - 1 - s) * pps + j], 0, 0)
        return pl.BlockSpec((None, None, d, page), idx)

    in_specs = [vec, vec, vec] + [page_spec(j) for j in range(pps)] * 2
    lanes_eq = pltpu.VMEM((N_HEADS, page), F32)
    if moba:
        n_past = n_pages * page // MOBA_BLOCK
        stats = pltpu.VMEM((n_past, N_HEADS, page), F32)
        scratch = [pltpu.VMEM((d, page), F32), stats, stats, stats, pltpu.VMEM((n_past, d, page), F32)]
    else:
        scratch = [pltpu.VMEM((d, page), F32), lanes_eq, pltpu.VMEM((d, page), F32)]
    kern = functools.partial(_decode_kernel, pps=pps, page=page, moba=moba, n_steps=n_steps)
    return pl.pallas_call(
        kern,
        grid_spec=pltpu.PrefetchScalarGridSpec(
            num_scalar_prefetch=1, grid=(db, n_steps), in_specs=in_specs, out_specs=vec,
            scratch_shapes=scratch),
        out_shape=jax.ShapeDtypeStruct((db, 1, d), F32),
        compiler_params=pltpu.CompilerParams(dimension_semantics=("arbitrary", "arbitrary"),
                                             vmem_limit_bytes=VMEM_LIMIT),
        name="moba_decode" if moba else "sb_decode",
    )(page_table, q, k_new, v_new, *([cache_kt] * pps), *([cache_vt] * pps))


def _gelu_tanh(x):
    return 0.5 * x * (1.0 + jnp.tanh(math.sqrt(2.0 / math.pi) * (x + 0.044715 * (x * x * x))))


def _post_kernel(*refs, decode, alpha, d_ff, tiles_per_seq):
    if decode:
        (x_ref, o_ref, wo_ref, g1_ref, b1_ref, win_ref, cw_ref, cb_ref, wd_ref, g2_ref, b2_ref,
         st_ref, y_ref, cs_ref, h_ref) = refs
    else:
        (x_ref, o_ref, wo_ref, g1_ref, b1_ref, win_ref, cw_ref, cb_ref, wd_ref, g2_ref, b2_ref,
         y_ref, cs_ref, h_ref, tail_ref) = refs
        i = pl.program_id(0)

        @pl.when(i % tiles_per_seq == 0)
        def _():
            tail_ref[...] = jnp.zeros_like(tail_ref)

    x = x_ref[...]
    tm = x.shape[0]
    attn = _dot(o_ref[...].astype(BF16), wo_ref[...])
    x1 = _layer_norm(alpha * x + attn, g1_ref[...], b1_ref[...])
    x1b = x1.astype(BF16)
    for c in range(d_ff // FF_CHUNK):
        lo, hi = c * FF_CHUNK, (c + 1) * FF_CHUNK
        a = _dot(x1b, win_ref[:, lo:hi])
        g = _dot(x1b, win_ref[:, d_ff + lo:d_ff + hi])
        if decode:
            prev2 = st_ref[:, lo:hi]
            prev1 = st_ref[:, d_ff + lo:d_ff + hi]
            cs_ref[:, lo:hi] = prev1
            cs_ref[:, d_ff + lo:d_ff + hi] = a
        else:
            ext = jnp.concatenate([tail_ref[:, lo:hi], a], axis=0)
            prev1 = ext[SUBLANES - 1:SUBLANES - 1 + tm]
            prev2 = ext[SUBLANES - 2:SUBLANES - 2 + tm]
            last = a[tm - SUBLANES:]
            tail_ref[:, lo:hi] = last
            cs_ref[:, lo:hi] = last
        conv = cb_ref[:, lo:hi] + cw_ref[0:1, lo:hi] * prev2 + cw_ref[1:2, lo:hi] * prev1 + cw_ref[2:3, lo:hi] * a
        h_ref[:, lo:hi] = (_gelu_tanh(conv) * g).astype(BF16)
    f = _dot(h_ref[...], wd_ref[...])
    y_ref[...] = _layer_norm(alpha * x1 + f, g2_ref[...], b2_ref[...])


def _post_attention(x, o, w_o, ln1_g, ln1_b, w_in, conv_w, conv_b, w_down, ln2_g, ln2_b, state,
                    *, alpha, tm, seq_rows):
    m, d = x.shape
    d_ff = w_down.shape[0]
    decode = state is not None
    grid = (m // tm,)
    row = lambda i: (i, 0)
    const = lambda i: (0, 0)

    def resident(shape):
        return pl.BlockSpec(shape, const, pipeline_mode=pl.Buffered(1))

    in_specs = [pl.BlockSpec((tm, d), row), pl.BlockSpec((tm, d), row),
                resident((d, d)), resident((1, d)), resident((1, d)),
                resident((d, 2 * d_ff)), resident((CONV_W, d_ff)), resident((1, d_ff)),
                resident((d_ff, d)), resident((1, d)), resident((1, d))]
    args = [x, o, w_o, ln1_g, ln1_b, w_in, conv_w, conv_b, w_down, ln2_g, ln2_b]
    scratch = [pltpu.VMEM((tm, d_ff), BF16)]
    if decode:
        in_specs.append(pl.BlockSpec((tm, 2 * d_ff), row))
        args.append(state)
        cs_shape = jax.ShapeDtypeStruct((m, 2 * d_ff), F32)
        cs_spec = pl.BlockSpec((tm, 2 * d_ff), row)
    else:
        tiles_per_seq = seq_rows // tm
        cs_shape = jax.ShapeDtypeStruct((m // seq_rows, SUBLANES, d_ff), F32)
        cs_spec = pl.BlockSpec((None, SUBLANES, d_ff), lambda i: (i // tiles_per_seq, 0, 0))
        scratch.append(pltpu.VMEM((SUBLANES, d_ff), F32))
    kern = functools.partial(_post_kernel, decode=decode, alpha=alpha, d_ff=d_ff,
                             tiles_per_seq=None if decode else seq_rows // tm)
    return pl.pallas_call(
        kern, grid=grid, in_specs=in_specs,
        out_specs=[pl.BlockSpec((tm, d), row), cs_spec],
        out_shape=[jax.ShapeDtypeStruct((m, d), F32), cs_shape],
        scratch_shapes=scratch,
        compiler_params=pltpu.CompilerParams(dimension_semantics=("arbitrary",),
                                             vmem_limit_bytes=VMEM_LIMIT),
        name="post_attn_ffn_decode" if decode else "post_attn_ffn",
    )(*args)


def kernel(x_prompt, x_sample, cache_k, cache_v, state_conv, page_table, w_qkv, w_o, ln1_g, ln1_b,
           w_in, conv_w, conv_b, w_down, ln2_g, ln2_b):
    batch, seq, d = x_prompt.shape
    db, dec_seq, _ = x_sample.shape
    depth, n_pool, page = cache_k.shape[:3]
    n_pages = page_table.shape[1]
    past_len = n_pages * page
    d_ff = w_down.shape[1]
    head_dim = d // N_HEADS
    assert dec_seq == 1 and 2 * head_dim == LANES and d_ff % FF_CHUNK == 0 and page == LANES
    assert seq % MOBA_BLOCK == 0 and seq % ROW_TILE == 0 and ROW_TILE % MOBA_BLOCK == 0
    assert past_len % MOBA_BLOCK == 0 and MOBA_BLOCK % page == 0 and db % SUBLANES == 0
    assert (seq // MOBA_BLOCK) % MOBA_BLOCKS_PER_TRIP == 0 and seq // MOBA_BLOCK <= LANES // 2
    assert n_pages % DECODE_PAGES_PER_STEP == 0 and (DECODE_PAGES_PER_STEP * page) % MOBA_BLOCK == 0
    alpha = (2.0 * depth) ** 0.25

    xp = x_prompt.reshape(batch * seq, d)
    xs = x_sample.reshape(db, d)
    ckt = jnp.transpose(cache_k, (0, 1, 3, 4, 2)).reshape(depth, n_pool, d, page)
    cvt = jnp.transpose(cache_v, (0, 1, 3, 4, 2)).reshape(depth, n_pool, d, page)
    st = state_conv.reshape(depth, db, (CONV_W - 1) * d_ff)
    tables_p = _rope_tables(jnp.arange(seq, dtype=jnp.int32), head_dim)
    tables_s = _rope_tables(jnp.full((db,), past_len, jnp.int32), head_dim)
    row2 = lambda a: a.reshape(1, -1)

    cp_l, ks_l, vs_l, cs_l = [], [], [], []
    kv_bufs = None
    for l in range(depth):
        moba = l % N_MIXERS == 1
        wq = w_qkv[l].astype(BF16)
        post_w = (w_o[l].astype(BF16), row2(ln1_g[l]), row2(ln1_b[l]), w_in[l].astype(BF16), conv_w[l],
                  row2(conv_b[l]), w_down[l].astype(BF16), row2(ln2_g[l]), row2(ln2_b[l]))
        outs = _qkv_proj(xp, wq, tables_p, kv_bufs, rope=moba, moba=moba, decode=False, tm=ROW_TILE,
                         layer=l, depth=depth, seq=seq)
        if moba:
            q_hi, q_lo, kt, vt, kb, vb, km = outs
            o = _prompt_attention((q_hi, q_lo), kb, vb, km.reshape(batch, seq // MOBA_BLOCK, d),
                                  batch=batch, seq=seq, moba=True)
        else:
            q_hi, kt, vt, kb, vb = outs
            o = _prompt_attention((q_hi,), kb, vb, None, batch=batch, seq=seq, moba=False)
        kv_bufs = (kt, vt)
        xp, tail = _post_attention(xp, o, *post_w, None, alpha=alpha, tm=ROW_TILE, seq_rows=seq)
        cp_l.append(tail[:, SUBLANES - (CONV_W - 1):, :])
        qs, ksn, vsn = _qkv_proj(xs, wq, tables_s, rope=moba, moba=moba, decode=True, tm=db)
        os_ = _decode_attention(l, qs.reshape(db, 1, d), ksn.reshape(db, 1, d), vsn.reshape(db, 1, d),
                                ckt, cvt, page_table, moba=moba)
        xs, cs = _post_attention(xs, os_.reshape(db, d), *post_w, st[l], alpha=alpha, tm=db, seq_rows=1)
        ks_l.append(ksn.reshape(db, 1, N_HEADS, head_dim))
        vs_l.append(vsn.reshape(db, 1, N_HEADS, head_dim))
        cs_l.append(cs.reshape(db, CONV_W - 1, d_ff))
    new_k, new_v = (jnp.transpose(t.reshape(depth, batch, N_HEADS, head_dim, seq), (0, 1, 4, 2, 3))
                    for t in kv_bufs)
    return (xp.reshape(batch, seq, d), xs.reshape(db, 1, d), new_k, new_v,
            jnp.stack(cp_l), jnp.stack(ks_l), jnp.stack(vs_l), jnp.stack(cs_l))
```
